```python
import functools
import math
import jax
import jax.numpy as jnp
from jax import lax
import numpy as np

D_MODEL = 2048
BATCH = 4
SEQ = 2048
DEPTH = 1
DEC_BATCH = 128
DEC_SEQ = 4
PAST_LEN = 16384
PAGE_SIZE = 128

D_SSM = D_MODEL // 2
SSM_GROUP = 16
N_GROUPS = D_SSM // SSM_GROUP
SSM_STATE = 64
DT_MIN = 0.001
DT_MAX = 0.1
N_HEADS = 16
QK_NOPE = 128
QK_ROPE = 64
V_DIM = 128
Q_RANK = 512
KV_RANK = 256
ROPE_THETA = 10000.0
Q_BLOCK = 128
MASK_VALUE = -1e30
N_EXPERTS = 32
TOP_K = 4
D_EXPERT = D_MODEL
SWIGLU_LIMIT = 7.0
SWIGLU_ALPHA = 1.702
EPS = 1e-6
SPLITS = (D_SSM, D_SSM + Q_RANK, D_SSM + Q_RANK + KV_RANK, D_SSM + Q_RANK + KV_RANK + QK_ROPE)
IN_COLS = SPLITS[-1] + 2 * D_MODEL

kernel_name = 'hybrid_s5_mla_moe_adaln_step'


def rmsnorm(x, g):
    x32 = x.astype(jnp.float32)
    y = x32 * lax.rsqrt(jnp.mean(x32 * x32, axis=-1, keepdims=True) + EPS)
    return (y * g.astype(jnp.float32)).astype(x.dtype)


def rope(x, pos):
    half = x.shape[-1] // 2
    inv = ROPE_THETA ** (-jnp.arange(half, dtype=jnp.float32) / half)
    ang = pos.astype(jnp.float32)[:, None] * inv
    ang = ang.reshape(ang.shape[:1] + (1,) * (x.ndim - 3) + (half,))
    cos, sin = jnp.cos(ang), jnp.sin(ang)
    x32 = x.astype(jnp.float32)
    x1, x2 = x32[..., :half], x32[..., half:]
    return jnp.concatenate([x1 * cos - x2 * sin, x2 * cos + x1 * sin], axis=-1).astype(x.dtype)


def ssm_branch(u, h0_re, h0_im, lam_re, lam_im, log_dt, b_re, b_im, c_re, c_im, d_skip):
    f32 = jnp.float32
    bsz, slen, _ = u.shape
    lam = lax.complex(lam_re.astype(f32), lam_im.astype(f32))
    dt = jnp.exp(log_dt.astype(f32))[:, None]
    lam_bar = jnp.exp(lam * dt)
    b_bar = ((lam_bar - 1.0) / lam)[:, :, None] * lax.complex(b_re.astype(f32), b_im.astype(f32))
    uc = u.astype(f32).reshape(bsz, slen, N_GROUPS, SSM_GROUP)
    bu = jnp.einsum('gnc,bsgc->bsgn', b_bar, uc.astype(jnp.complex64))
    h0 = lax.complex(h0_re.astype(f32), h0_im.astype(f32))
    bu = bu.at[:, 0].add(lam_bar * h0)
    a = jnp.broadcast_to(lam_bar, bu.shape)

    def combine(left, right):
        return (right[0] * left[0], right[0] * left[1] + right[1])

    _, hs = lax.associative_scan(combine, (a, bu), axis=1)
    c = lax.complex(c_re.astype(f32), c_im.astype(f32))
    y = jnp.real(jnp.einsum('gcn,bsgn->bsgc', c, hs)) + d_skip.astype(f32).reshape(N_GROUPS, SSM_GROUP) * uc
    h_last = hs[:, -1]
    return (y.reshape(bsz, slen, D_SSM).astype(u.dtype),
            jnp.real(h_last).astype(h0_re.dtype), jnp.imag(h_last).astype(h0_re.dtype))


def mla_attend(q_lat, q_pe, ckv, kpe, q_pos, k_pos):
    scale = (QK_NOPE + QK_ROPE) ** -0.5
    s = (jnp.einsum('bqhr,bkr->bhqk', q_lat, ckv, preferred_element_type=jnp.float32)
         + jnp.einsum('bqhd,bkd->bhqk', q_pe, kpe, preferred_element_type=jnp.float32)) * scale
    s = jnp.where(k_pos[None, :] <= q_pos[:, None], s, MASK_VALUE)
    p = jax.nn.softmax(s, axis=-1).astype(ckv.dtype)
    return jnp.einsum('bhqk,bkr->bqhr', p, ckv)


def attend_prompt(q_lat, q_pe, ckv, kpe):
    bsz, slen = q_lat.shape[:2]
    qb = min(Q_BLOCK, slen)
    k_pos = jnp.arange(slen, dtype=jnp.int32)

    def block(i):
        st = i * qb
        ql = lax.dynamic_slice_in_dim(q_lat, st, qb, axis=1)
        qp = lax.dynamic_slice_in_dim(q_pe, st, qb, axis=1)
        q_pos = st + jnp.arange(qb, dtype=jnp.int32)
        return mla_attend(ql, qp, ckv, kpe, q_pos, k_pos)

    o = lax.map(block, jnp.arange(slen // qb, dtype=jnp.int32))
    return jnp.transpose(o, (1, 0, 2, 3, 4)).reshape(bsz, slen, N_HEADS, KV_RANK)


def attend_cached(q_lat, q_pe, ckv_new, kpe_new, ckv_past, kpe_past):
    past, t = ckv_past.shape[1], ckv_new.shape[1]
    ckv = jnp.concatenate([ckv_past.astype(ckv_new.dtype), ckv_new], axis=1)
    kpe = jnp.concatenate([kpe_past.astype(kpe_new.dtype), kpe_new], axis=1)
    q_pos = past + jnp.arange(t, dtype=jnp.int32)
    k_pos = jnp.arange(past + t, dtype=jnp.int32)
    return mla_attend(q_lat, q_pe, ckv, kpe, q_pos, k_pos)


def moe(h, w_router, b_router, w_gu, b_gu, w_down, b_down):
    shp = h.shape
    t = h.reshape(-1, shp[-1])
    n = t.shape[0]
    logits = (t @ w_router + b_router).astype(jnp.float32)
    top_v, top_i = lax.top_k(logits, TOP_K)
    wts = jax.nn.softmax(top_v, axis=-1)
    flat_e = top_i.reshape(-1)
    order = jnp.argsort(flat_e)
    e_sorted = flat_e[order]
    tok = order // TOP_K
    sizes = jnp.bincount(flat_e, length=N_EXPERTS).astype(jnp.int32)
    xs = t[tok]
    gu = lax.ragged_dot(xs, w_gu, sizes) + b_gu[e_sorted]
    gate = jnp.minimum(gu[:, :D_EXPERT], SWIGLU_LIMIT)
    up = jnp.clip(gu[:, D_EXPERT:], -SWIGLU_LIMIT, SWIGLU_LIMIT)
    act = (up + 1.0) * gate * jax.nn.sigmoid(SWIGLU_ALPHA * gate)
    out = lax.ragged_dot(act, w_down, sizes) + b_down[e_sorted]
    out = out * wts.reshape(-1)[order][:, None].astype(out.dtype)
    return jax.ops.segment_sum(out, tok, num_segments=n).reshape(shp)


def hybrid_layer(x, c, pos, h0_re, h0_im, attend, p):
    bsz, slen, _ = x.shape
    mod = (jax.nn.silu(c) @ p['w_ada'] + p['b_ada'])[:, None, :]
    sh1, sc1, gt1, sh2, sc2, gt2 = jnp.split(mod, 6, axis=-1)
    h = rmsnorm(x, p['g_mix']) * (1.0 + sc1) + sh1
    z = h @ p['w_in']
    u, q_a, kv_a, k_pe, gates = jnp.split(z, SPLITS, axis=-1)
    y_ssm, h_re, h_im = ssm_branch(u, h0_re, h0_im, p['ssm_lam_re'], p['ssm_lam_im'], p['ssm_log_dt'],
                                   p['ssm_b_re'], p['ssm_b_im'], p['ssm_c_re'], p['ssm_c_im'], p['ssm_d'])
    g = jax.nn.gelu(y_ssm) @ p['w_glu']
    br_a = g[..., :D_MODEL] * jax.nn.sigmoid(g[..., D_MODEL:])
    q = (rmsnorm(q_a, p['q_norm_g']) @ p['w_qb']).reshape(bsz, slen, N_HEADS, QK_NOPE + QK_ROPE)
    q_pe = rope(q[..., QK_NOPE:], pos)
    q_lat = jnp.einsum('bshd,rhd->bshr', q[..., :QK_NOPE], p['w_uk'])
    ckv = rmsnorm(kv_a, p['kv_norm_g'])
    kpe = rope(k_pe, pos)
    o_lat = attend(q_lat, q_pe, ckv, kpe)
    o = jnp.einsum('bshr,rhd->bshd', o_lat, p['w_uv']).reshape(bsz, slen, N_HEADS * V_DIM)
    br_b = o @ p['w_bo']
    g_a, g_b = jnp.split(gates, 2, axis=-1)
    mixed = (jax.nn.sigmoid(g_a) * br_a + jax.nn.sigmoid(g_b) * br_b) @ p['w_o']
    x = x + gt1 * mixed
    h2 = rmsnorm(x, p['g_ffn']) * (1.0 + sc2) + sh2
    x = x + gt2 * moe(h2, p['w_router'], p['b_router'], p['w_gu'], p['b_gu'], p['w_down'], p['b_down'])
    return x, ckv, kpe, h_re, h_im


def setup_inputs(seed: int = 0) -> dict:
    key = jax.random.key(seed)
    ks = iter(jax.random.split(key, 48))
    f32 = jnp.float32
    L = DEPTH

    def nrm(shape, scale):
        return jax.random.normal(next(ks), shape, f32) * scale

    def gain(shape):
        return 1.0 + 0.01 * jax.random.normal(next(ks), shape, f32)

    n_pages = PAST_LEN // PAGE_SIZE
    n_pool = (DEC_BATCH * n_pages * 5) // 4
    page_table = jax.random.permutation(next(ks), n_pool)[: DEC_BATCH * n_pages].reshape(DEC_BATCH, n_pages).astype(jnp.int32)
    lam_im = jnp.broadcast_to(math.pi * jnp.arange(SSM_STATE, dtype=f32), (L, N_GROUPS, SSM_STATE))
    return {
        'x_prompt': nrm((BATCH, SEQ, D_MODEL), 1.0),
        'x_sample': nrm((DEC_BATCH, DEC_SEQ, D_MODEL), 1.0),
        'c_prompt': nrm((BATCH, D_MODEL), 1.0),
        'c_sample': nrm((DEC_BATCH, D_MODEL), 1.0),
        'cache_ckv': nrm((L, n_pool, PAGE_SIZE, KV_RANK), 1.0),
        'cache_kpe': nrm((L, n_pool, PAGE_SIZE, QK_ROPE), 1.0),
        'page_table': page_table,
        'state_ssm_re': nrm((L, DEC_BATCH, N_GROUPS, SSM_STATE), 0.1),
        'state_ssm_im': nrm((L, DEC_BATCH, N_GROUPS, SSM_STATE), 0.1),
        'w_ada': nrm((L, D_MODEL, 6 * D_MODEL), 0.5 * D_MODEL ** -0.5),
        'b_ada': nrm((L, 6 * D_MODEL), 0.02),
        'g_mix': gain((L, D_MODEL)),
        'w_in': nrm((L, D_MODEL, IN_COLS), D_MODEL ** -0.5),
        'ssm_lam_re': -0.5 + nrm((L, N_GROUPS, SSM_STATE), 0.01),
        'ssm_lam_im': lam_im + nrm((L, N_GROUPS, SSM_STATE), 0.01),
        'ssm_log_dt': jax.random.uniform(next(ks), (L, N_GROUPS), f32, math.log(DT_MIN), math.log(DT_MAX)),
        'ssm_b_re': nrm((L, N_GROUPS, SSM_STATE, SSM_GROUP), (2 * SSM_GROUP) ** -0.5),
        'ssm_b_im': nrm((L, N_GROUPS, SSM_STATE, SSM_GROUP), (2 * SSM_GROUP) ** -0.5),
        'ssm_c_re': nrm((L, N_GROUPS, SSM_GROUP, SSM_STATE), SSM_STATE ** -0.5),
        'ssm_c_im': nrm((L, N_GROUPS, SSM_GROUP, SSM_STATE), SSM_STATE ** -0.5),
        'ssm_d': nrm((L, D_SSM), 1.0),
        'w_glu': nrm((L, D_SSM, 2 * D_MODEL), D_SSM ** -0.5),
        'q_norm_g': gain((L, Q_RANK)),
        'kv_norm_g': gain((L, KV_RANK)),
        'w_qb': nrm((L, Q_RANK, N_HEADS * (QK_NOPE + QK_ROPE)), Q_RANK ** -0.5),
        'w_uk': nrm((L, KV_RANK, N_HEADS, QK_NOPE), KV_RANK ** -0.5),
        'w_uv': nrm((L, KV_RANK, N_HEADS, V_DIM), KV_RANK ** -0.5),
        'w_bo': nrm((L, N_HEADS * V_DIM, D_MODEL), (N_HEADS * V_DIM) ** -0.5),
        'w_o': nrm((L, D_MODEL, D_MODEL), D_MODEL ** -0.5),
        'g_ffn': gain((L, D_MODEL)),
        'w_router': nrm((L, D_MODEL, N_EXPERTS), D_MODEL ** -0.5),
        'b_router': nrm((L, N_EXPERTS), 0.01),
        'w_gu': nrm((L, N_EXPERTS, D_MODEL, 2 * D_EXPERT), D_MODEL ** -0.5),
        'b_gu': nrm((L, N_EXPERTS, 2 * D_EXPERT), 0.01),
        'w_down': nrm((L, N_EXPERTS, D_EXPERT, D_MODEL), D_EXPERT ** -0.5),
        'b_down': nrm((L, N_EXPERTS, D_MODEL), 0.01),
        'g_final': gain((D_MODEL,)),
    }


def reference(x_prompt, x_sample, c_prompt, c_sample, cache_ckv, cache_kpe, page_table,
              state_ssm_re, state_ssm_im, w_ada, b_ada, g_mix, w_in, ssm_lam_re, ssm_lam_im,
              ssm_log_dt, ssm_b_re, ssm_b_im, ssm_c_re, ssm_c_im, ssm_d, w_glu, q_norm_g,
              kv_norm_g, w_qb, w_uk, w_uv, w_bo, w_o, g_ffn, w_router, b_router, w_gu, b_gu,
              w_down, b_down, g_final):
    dec_batch = x_sample.shape[0]
    past_len = page_table.shape[1] * PAGE_SIZE
    pos_p = jnp.arange(x_prompt.shape[1], dtype=jnp.int32)
    pos_s = past_len + jnp.arange(x_sample.shape[1], dtype=jnp.int32)
    h0_p = jnp.zeros((x_prompt.shape[0], N_GROUPS, SSM_STATE), x_prompt.dtype)
    xp, xs = x_prompt, x_sample
    ckv_p, kpe_p, sre_p, sim_p = [], [], [], []
    ckv_s, kpe_s, sre_s, sim_s = [], [], [], []
    for l in range(DEPTH):
        p = {
            'w_ada': w_ada[l], 'b_ada': b_ada[l], 'g_mix': g_mix[l], 'w_in': w_in[l],
            'ssm_lam_re': ssm_lam_re[l], 'ssm_lam_im': ssm_lam_im[l], 'ssm_log_dt': ssm_log_dt[l],
            'ssm_b_re': ssm_b_re[l], 'ssm_b_im': ssm_b_im[l], 'ssm_c_re': ssm_c_re[l],
            'ssm_c_im': ssm_c_im[l], 'ssm_d': ssm_d[l], 'w_glu': w_glu[l],
            'q_norm_g': q_norm_g[l], 'kv_norm_g': kv_norm_g[l], 'w_qb': w_qb[l], 'w_uk': w_uk[l],
            'w_uv': w_uv[l], 'w_bo': w_bo[l], 'w_o': w_o[l], 'g_ffn': g_ffn[l],
            'w_router': w_router[l], 'b_router': b_router[l], 'w_gu': w_gu[l], 'b_gu': b_gu[l],
            'w_down': w_down[l], 'b_down': b_down[l],
        }
        xp, a1, a2, a3, a4 = hybrid_layer(xp, c_prompt, pos_p, h0_p, h0_p, attend_prompt, p)
        ckv_p.append(a1); kpe_p.append(a2); sre_p.append(a3); sim_p.append(a4)
        ckv_past = cache_ckv[l, page_table].reshape(dec_batch, past_len, KV_RANK)
        kpe_past = cache_kpe[l, page_table].reshape(dec_batch, past_len, QK_ROPE)
        attend_s = functools.partial(attend_cached, ckv_past=ckv_past, kpe_past=kpe_past)
        xs, b1, b2, b3, b4 = hybrid_layer(xs, c_sample, pos_s, state_ssm_re[l], state_ssm_im[l], attend_s, p)
        ckv_s.append(b1); kpe_s.append(b2); sre_s.append(b3); sim_s.append(b4)
    y_prompt = rmsnorm(xp, g_final)
    y_sample = rmsnorm(xs, g_final)
    return (y_prompt, y_sample,
            jnp.stack(ckv_p), jnp.stack(kpe_p), jnp.stack(sre_p), jnp.stack(sim_p),
            jnp.stack(ckv_s), jnp.stack(kpe_s), jnp.stack(sre_s), jnp.stack(sim_s))
```

```python
import functools
import math

import jax
import jax.numpy as jnp
from jax import lax
from jax.experimental import pallas as pl
from jax.experimental.pallas import tpu as pltpu

F32 = jnp.float32
BF16 = jnp.bfloat16

SSM_GROUP = 16
SSM_STATE = 64
N_HEADS = 16
QK_NOPE = 128
QK_ROPE = 64
V_DIM = 128
Q_RANK = 512
KV_RANK = 256
ROPE_THETA = 10000.0
PAGE_SIZE = 128
MASK_VALUE = -1e30
N_EXPERTS = 32
TOP_K = 4
SWIGLU_LIMIT = 7.0
SWIGLU_ALPHA = 1.702
EPS = 1e-6

LANES = 128
SUBLANES = 8
VMEM_LIMIT = 56 * 1024 * 1024

ROW_TILE = 128
MM_TM = 512
SSM_CHUNK = 256
SSM_SUB = SSM_CHUNK // SUBLANES
SSM_GB = 8
ATT_TQ = 64
ATT_TK = 256
DEC_PAGES = 8
MOE_TM = 256
MOE_TN = 512


def _cparams(sem):
    return pltpu.CompilerParams(dimension_semantics=sem, vmem_limit_bytes=VMEM_LIMIT)


def _sigmoid(x):
    return 1.0 / (1.0 + jnp.exp(-x))


def _gelu_tanh(x):
    return 0.5 * x * (1.0 + jnp.tanh(math.sqrt(2.0 / math.pi) * (x + 0.044715 * (x * x * x))))


def _dot(a, b):
    return jnp.dot(a, b, preferred_element_type=F32)


def _dot_nt(a, b):
    return lax.dot_general(a, b, (((1,), (1,)), ((), ())), preferred_element_type=F32)


def _ada_body(c_ref, w_ref, b_ref, o_ref):
    c = c_ref[...]
    a = (c * _sigmoid(c)).astype(BF16)
    o_ref[...] = _dot(a, w_ref[...]) + b_ref[...]


def _ada(c_all, w_ada_b, b_ada):
    m, d = c_all.shape
    n = w_ada_b.shape[1]
    tn = 1024
    return pl.pallas_call(
        _ada_body,
        grid=(n // tn,),
        in_specs=[pl.BlockSpec((m, d), lambda j: (0, 0)),
                  pl.BlockSpec((d, tn), lambda j: (0, j)),
                  pl.BlockSpec((1, tn), lambda j: (0, j))],
        out_specs=pl.BlockSpec((m, tn), lambda j: (0, j)),
        out_shape=jax.ShapeDtypeStruct((m, n), F32),
        compiler_params=_cparams(("arbitrary",)),
        name="ada_mod",
    )(c_all, w_ada_b, b_ada.reshape(1, n))


def _mod_spec(chunk, d, tiles_per_batch, n_batch):
    return pl.BlockSpec((None, ROW_TILE, d),
                        lambda i: (jnp.minimum(i // tiles_per_batch, n_batch), 0, chunk))


def _norm_mod_body(x_ref, g_ref, sh_ref, sc_ref, o_ref):
    x = x_ref[...]
    y = x * lax.rsqrt(jnp.mean(x * x, axis=-1, keepdims=True) + EPS)
    o_ref[...] = ((y * g_ref[...]) * (1.0 + sc_ref[...]) + sh_ref[...]).astype(o_ref.dtype)


def _norm_mod(x, g, mod_all, tiles_per_batch, n_batch):
    n, d = x.shape
    return pl.pallas_call(
        _norm_mod_body,
        grid=(n // ROW_TILE,),
        in_specs=[pl.BlockSpec((ROW_TILE, d), lambda i: (i, 0)),
                  pl.BlockSpec((1, d), lambda i: (0, 0)),
                  _mod_spec(0, d, tiles_per_batch, n_batch),
                  _mod_spec(1, d, tiles_per_batch, n_batch)],
        out_specs=pl.BlockSpec((ROW_TILE, d), lambda i: (i, 0)),
        out_shape=jax.ShapeDtypeStruct((n, d), BF16),
        compiler_params=_cparams(("arbitrary",)),
        name="norm_mod1",
    )(x, g.reshape(1, d), mod_all, mod_all)


def _resid_norm_mod_body(x_ref, m_ref, g_ref, gt_ref, sh_ref, sc_ref, x1_ref, h_ref):
    x = x_ref[...] + gt_ref[...] * m_ref[...]
    x1_ref[...] = x
    y = x * lax.rsqrt(jnp.mean(x * x, axis=-1, keepdims=True) + EPS)
    h_ref[...] = (y * g_ref[...]) * (1.0 + sc_ref[...]) + sh_ref[...]


def _resid_norm_mod(x, mixed, g, mod_all, tiles_per_batch, n_batch):
    n, d = x.shape
    row = pl.BlockSpec((ROW_TILE, d), lambda i: (i, 0))
    return pl.pallas_call(
        _resid_norm_mod_body,
        grid=(n // ROW_TILE,),
        in_specs=[row, row, pl.BlockSpec((1, d), lambda i: (0, 0)),
                  _mod_spec(2, d, tiles_per_batch, n_batch),
                  _mod_spec(3, d, tiles_per_batch, n_batch),
                  _mod_spec(4, d, tiles_per_batch, n_batch)],
        out_specs=[row, row],
        out_shape=[jax.ShapeDtypeStruct((n, d), F32), jax.ShapeDtypeStruct((n, d), F32)],
        compiler_params=_cparams(("arbitrary",)),
        name="resid_norm_mod2",
    )(x, mixed, g.reshape(1, d), mod_all, mod_all, mod_all)


def _final_body(x_ref, m_ref, g_ref, gt_ref, o_ref):
    x = x_ref[...] + gt_ref[...] * m_ref[...]
    y = x * lax.rsqrt(jnp.mean(x * x, axis=-1, keepdims=True) + EPS)
    o_ref[...] = y * g_ref[...]


def _final(x1, moe_out, g, mod_all, tiles_per_batch, n_batch):
    n, d = x1.shape
    row = pl.BlockSpec((ROW_TILE, d), lambda i: (i, 0))
    return pl.pallas_call(
        _final_body,
        grid=(n // ROW_TILE,),
        in_specs=[row, row, pl.BlockSpec((1, d), lambda i: (0, 0)),
                  _mod_spec(5, d, tiles_per_batch, n_batch)],
        out_specs=row,
        out_shape=jax.ShapeDtypeStruct((n, d), F32),
        compiler_params=_cparams(("arbitrary",)),
        name="final_norm",
    )(x1, moe_out, g.reshape(1, d), mod_all)


def _mm_body(a_ref, w_ref, o_ref):
    o_ref[...] = _dot(a_ref[...].astype(BF16), w_ref[...]).astype(o_ref.dtype)


def _matmul(a, w, *, tn, out_dtype, name):
    m, k = a.shape
    n = w.shape[1]
    tm = MM_TM
    return pl.pallas_call(
        _mm_body,
        grid=(n // tn, m // tm),
        in_specs=[pl.BlockSpec((tm, k), lambda j, i: (i, 0)),
                  pl.BlockSpec((k, tn), lambda j, i: (0, j))],
        out_specs=pl.BlockSpec((tm, tn), lambda j, i: (i, j)),
        out_shape=jax.ShapeDtypeStruct((m, n), out_dtype),
        compiler_params=_cparams(("arbitrary", "arbitrary")),
        name=name,
    )(a, w)


def _glu_body(a_ref, w1_ref, w2_ref, ga_ref, o_ref):
    a = a_ref[...].astype(BF16)
    g1 = _dot(a, w1_ref[...])
    g2 = _dot(a, w2_ref[...])
    o_ref[...] = _sigmoid(ga_ref[...]) * (g1 * _sigmoid(g2))


def _glu_branch(gy, w_glu_b, z, ga_col):
    m, k = gy.shape
    d = w_glu_b.shape[1] // 2
    tm, tn = MM_TM, 512
    nb = d // tn
    return pl.pallas_call(
        _glu_body,
        grid=(nb, m // tm),
        in_specs=[pl.BlockSpec((tm, k), lambda j, i: (i, 0)),
                  pl.BlockSpec((k, tn), lambda j, i: (0, j)),
                  pl.BlockSpec((k, tn), lambda j, i: (0, j + nb)),
                  pl.BlockSpec((tm, tn), lambda j, i: (i, j + ga_col // tn))],
        out_specs=pl.BlockSpec((tm, tn), lambda j, i: (i, j)),
        out_shape=jax.ShapeDtypeStruct((m, d), F32),
        compiler_params=_cparams(("arbitrary", "arbitrary")),
        name="glu_branch",
    )(gy, w_glu_b, w_glu_b, z)


def _bo_mix_body(o_ref_in, w_ref, a_ref, gb_ref, out_ref):
    br_b = _dot(o_ref_in[...], w_ref[...])
    out_ref[...] = (a_ref[...] + _sigmoid(gb_ref[...]) * br_b).astype(out_ref.dtype)


def _bo_mix(o, w_bo_b, gated_a, z, gb_col):
    m, k = o.shape
    d = w_bo_b.shape[1]
    tm, tn = MM_TM, 512
    return pl.pallas_call(
        _bo_mix_body,
        grid=(d // tn, m // tm),
        in_specs=[pl.BlockSpec((tm, k), lambda j, i: (i, 0)),
                  pl.BlockSpec((k, tn), lambda j, i: (0, j)),
                  pl.BlockSpec((tm, tn), lambda j, i: (i, j)),
                  pl.BlockSpec((tm, tn), lambda j, i: (i, j + gb_col // tn))],
        out_specs=pl.BlockSpec((tm, tn), lambda j, i: (i, j)),
        out_shape=jax.ShapeDtypeStruct((m, d), BF16),
        compiler_params=_cparams(("arbitrary", "arbitrary")),
        name="bo_mix",
    )(o, w_bo_b, gated_a, z)


def _ssm_prompt_body(u_ref, wb_ref, wc_ref, d_ref, lam_ref, lamr_ref, pw_ref,
                     y_ref, st_ref, bu_s, hb_s, up_s, yp_s, hin_s, carry_s):
    w = lam_ref.shape[1] // 2
    ch = pl.program_id(2)

    @pl.when(ch == 0)
    def _():
        carry_s[...] = jnp.zeros_like(carry_s)

    for r in range(SSM_SUB):
        up_s[pl.ds(r * SUBLANES, SUBLANES), :] = u_ref[pl.ds(r, SUBLANES, stride=SSM_SUB), :]
    up = up_s[...]
    bu_s[...] = _dot(up.astype(BF16), wb_ref[...])

    lam_re = lam_ref[:, :w]
    lam_im = lam_ref[:, w:]

    def step(r, s):
        s_re, s_im = s
        row = pl.multiple_of(r * SUBLANES, SUBLANES)
        n_re = lam_re * s_re - lam_im * s_im + bu_s[pl.ds(row, SUBLANES), pl.ds(0, w)]
        n_im = lam_re * s_im + lam_im * s_re + bu_s[pl.ds(row, SUBLANES), pl.ds(w, w)]
        bu_s[pl.ds(row, SUBLANES), pl.ds(0, w)] = n_re
        bu_s[pl.ds(row, SUBLANES), pl.ds(w, w)] = n_im
        return n_re, n_im

    zero = jnp.zeros((SUBLANES, w), F32)
    e_re, e_im = lax.fori_loop(0, SSM_SUB, step, (zero, zero))

    lr_re = lamr_ref[0:1, :w]
    lr_im = lamr_ref[0:1, w:]
    h_re = carry_s[0:1, :w]
    h_im = carry_s[0:1, w:]
    for c in range(SUBLANES):
        hin_s[c:c + 1, :w] = h_re
        hin_s[c:c + 1, w:] = h_im
        n_re = lr_re * h_re - lr_im * h_im + e_re[c:c + 1, :]
        n_im = lr_re * h_im + lr_im * h_re + e_im[c:c + 1, :]
        h_re, h_im = n_re, n_im
    carry_s[0:1, :w] = h_re
    carry_s[0:1, w:] = h_im
    st_ref[0:1, :w] = h_re
    st_ref[0:1, w:] = h_im

    hin_re = hin_s[:, :w]
    hin_im = hin_s[:, w:]

    def fix(r, _):
        row = pl.multiple_of(r * SUBLANES, SUBLANES)
        p_re = pw_ref[pl.ds(row, SUBLANES), pl.ds(0, w)]
        p_im = pw_ref[pl.ds(row, SUBLANES), pl.ds(w, w)]
        f_re = bu_s[pl.ds(row, SUBLANES), pl.ds(0, w)] + (p_re * hin_re - p_im * hin_im)
        f_im = bu_s[pl.ds(row, SUBLANES), pl.ds(w, w)] + (p_re * hin_im + p_im * hin_re)
        hb_s[pl.ds(row, SUBLANES), pl.ds(0, w)] = f_re
        hb_s[pl.ds(row, SUBLANES), pl.ds(w, w)] = f_im
        return 0

    lax.fori_loop(0, SSM_SUB, fix, 0)

    y = _dot(hb_s[...].astype(BF16), wc_ref[...]) + d_ref[...] * up
    yp_s[...] = _gelu_tanh(y)
    for r in range(SSM_SUB):
        y_ref[pl.ds(r, SUBLANES, stride=SSM_SUB), :] = yp_s[pl.ds(r * SUBLANES, SUBLANES), :]


def _ssm_prompt(z, n_batch, seq, wb, wc, d_skip, lam8, lamr8, pw):
    nblk, cin, w2 = wb.shape
    n_chunks = seq // SSM_CHUNK
    grid = (n_batch, nblk, n_chunks)
    return pl.pallas_call(
        _ssm_prompt_body,
        grid=grid,
        in_specs=[pl.BlockSpec((SSM_CHUNK, cin), lambda b, k, c: (b * n_chunks + c, k)),
                  pl.BlockSpec((None, cin, w2), lambda b, k, c: (k, 0, 0)),
                  pl.BlockSpec((None, w2, cin), lambda b, k, c: (k, 0, 0)),
                  pl.BlockSpec((1, cin), lambda b, k, c: (0, k)),
                  pl.BlockSpec((None, SUBLANES, w2), lambda b, k, c: (k, 0, 0)),
                  pl.BlockSpec((None, SUBLANES, w2), lambda b, k, c: (k, 0, 0)),
                  pl.BlockSpec((None, SSM_CHUNK, w2), lambda b, k, c: (k, 0, 0))],
        out_specs=[pl.BlockSpec((SSM_CHUNK, cin), lambda b, k, c: (b * n_chunks + c, k)),
                   pl.BlockSpec((None, None, 1, w2), lambda b, k, c: (b, k, 0, 0))],
        out_shape=[jax.ShapeDtypeStruct((n_batch * seq, nblk * cin), F32),
                   jax.ShapeDtypeStruct((n_batch, nblk, 1, w2), F32)],
        scratch_shapes=[pltpu.VMEM((SSM_CHUNK, w2), F32),
                        pltpu.VMEM((SSM_CHUNK, w2), F32),
                        pltpu.VMEM((SSM_CHUNK, cin), F32),
                        pltpu.VMEM((SSM_CHUNK, cin), F32),
                        pltpu.VMEM((SUBLANES, w2), F32),
                        pltpu.VMEM((SUBLANES, w2), F32)],
        compiler_params=_cparams(("arbitrary", "arbitrary", "arbitrary")),
        name="ssm_prompt",
    )(z, wb, wc, d_skip, lam8, lamr8, pw)


def _ssm_sample_body(u_ref, wb_ref, wc_ref, d_ref, lam_ref, hre_ref, him_ref,
                     y_ref, ore_ref, oim_ref, bu_s, *, n_tok, n_seq):
    w = lam_ref.shape[1] // 2
    u = u_ref[...]
    bu_s[...] = _dot(u.astype(BF16), wb_ref[...])
    lam_re = lam_ref[0:1, :w]
    lam_im = lam_ref[0:1, w:]
    s_re = hre_ref[...]
    s_im = him_ref[...]
    for t in range(n_tok):
        rows = pl.ds(t * n_seq, n_seq)
        n_re = lam_re * s_re - lam_im * s_im + bu_s[rows, pl.ds(0, w)]
        n_im = lam_re * s_im + lam_im * s_re + bu_s[rows, pl.ds(w, w)]
        bu_s[rows, pl.ds(0, w)] = n_re
        bu_s[rows, pl.ds(w, w)] = n_im
        s_re, s_im = n_re, n_im
    ore_ref[...] = s_re
    oim_ref[...] = s_im
    y = _dot(bu_s[...].astype(BF16), wc_ref[...]) + d_ref[...] * u
    y_ref[...] = _gelu_tanh(y)


def _ssm_sample(z, row0, n_tok, n_seq, wb, wc, d_skip, lam8, h0_re, h0_im):
    nblk, cin, w2 = wb.shape
    w = w2 // 2
    rows = n_tok * n_seq
    rb = row0 // rows
    st = pl.BlockSpec((n_seq, w), lambda k: (0, k))
    return pl.pallas_call(
        functools.partial(_ssm_sample_body, n_tok=n_tok, n_seq=n_seq),
        grid=(nblk,),
        in_specs=[pl.BlockSpec((rows, cin), lambda k: (rb, k)),
                  pl.BlockSpec((None, cin, w2), lambda k: (k, 0, 0)),
                  pl.BlockSpec((None, w2, cin), lambda k: (k, 0, 0)),
                  pl.BlockSpec((1, cin), lambda k: (0, k)),
                  pl.BlockSpec((None, SUBLANES, w2), lambda k: (k, 0, 0)),
                  st, st],
        out_specs=[pl.BlockSpec((rows, cin), lambda k: (0, k)), st, st],
        out_shape=[jax.ShapeDtypeStruct((rows, nblk * cin), F32),
                   jax.ShapeDtypeStruct((n_seq, nblk * w), F32),
                   jax.ShapeDtypeStruct((n_seq, nblk * w), F32)],
        scratch_shapes=[pltpu.VMEM((rows, w2), F32)],
        compiler_params=_cparams(("arbitrary",)),
        name="ssm_sample",
    )(z, wb, wc, d_skip, lam8, h0_re, h0_im)


def _ssm_params(lam_re, lam_im, log_dt, b_re, b_im, c_re, c_im):
    g, n = lam_re.shape
    nblk = g // SSM_GB
    w = SSM_GB * n
    lam = lax.complex(lam_re, lam_im)
    dt = jnp.exp(log_dt)[:, None]
    lam_bar = jnp.exp(lam * dt)
    b_bar = ((lam_bar - 1.0) / lam)[:, :, None] * lax.complex(b_re, b_im)
    eye = jnp.eye(SSM_GB, dtype=F32)

    def blockdiag_in(x):
        x = x.reshape(nblk, SSM_GB, n, SSM_GROUP).transpose(0, 1, 3, 2)
        y = x[:, :, :, None, :] * eye[None, :, None, :, None]
        return y.reshape(nblk, SSM_GB * SSM_GROUP, w)

    def blockdiag_out(x):
        x = x.reshape(nblk, SSM_GB, SSM_GROUP, n).transpose(0, 3, 1, 2)
        y = x[:, None, :, :, :] * eye.T[None, :, None, :, None]
        return y.reshape(nblk, w, SSM_GB * SSM_GROUP)

    wb = jnp.concatenate([blockdiag_in(jnp.real(b_bar)), blockdiag_in(jnp.imag(b_bar))], axis=2)
    wc = jnp.concatenate([blockdiag_out(c_re), blockdiag_out(-c_im)], axis=1)

    def pack(x):
        lead = x.shape[:-2]
        x = x.reshape(lead + (nblk, w))
        return jnp.concatenate([jnp.real(x), jnp.imag(x)], axis=-1)

    lam_p = pack(lam_bar)
    lam8 = jnp.broadcast_to(lam_p[:, None, :], (nblk, SUBLANES, 2 * w))
    lam_sub = pack(jnp.exp(lam * dt * float(SSM_SUB)))
    lamr8 = jnp.broadcast_to(lam_sub[:, None, :], (nblk, SUBLANES, 2 * w))
    steps = jnp.arange(1, SSM_SUB + 1, dtype=F32)[:, None, None]
    pw = pack(jnp.exp((lam * dt)[None] * steps))
    pw = jnp.broadcast_to(pw.transpose(1, 0, 2)[:, :, None, :], (nblk, SSM_SUB, SUBLANES, 2 * w))
    pw = pw.reshape(nblk, SSM_CHUNK, 2 * w)
    return wb.astype(BF16), wc.astype(BF16), lam8, lamr8, pw


def _q_proj_body(qa_ref, g_ref, wn_ref, wpa_ref, wpb_ref, wuk_ref, tc_ref, ts_ref, ql_ref, qp_ref):
    x = qa_ref[...]
    xn = (x * lax.rsqrt(jnp.mean(x * x, axis=-1, keepdims=True) + EPS) * g_ref[...]).astype(BF16)
    q_nope = _dot(xn, wn_ref[...]).astype(BF16)
    ql_ref[:, :KV_RANK] = _dot(q_nope[:, :QK_NOPE], wuk_ref[0]).astype(ql_ref.dtype)
    ql_ref[:, KV_RANK:] = _dot(q_nope[:, QK_NOPE:], wuk_ref[1]).astype(ql_ref.dtype)
    pe = _dot(xn, wpa_ref[...]) * tc_ref[...] + _dot(xn, wpb_ref[...]) * ts_ref[...]
    lane = lax.broadcasted_iota(jnp.int32, pe.shape, 1)
    low = lane < QK_ROPE
    qp_ref[:, :LANES] = jnp.where(low, pe, 0.0).astype(qp_ref.dtype)
    qp_ref[:, LANES:] = jnp.where(low, pltpu.roll(pe, QK_ROPE, 1), 0.0).astype(qp_ref.dtype)


def _q_proj(z, qa_col, q_norm_g, w_nope, w_pe_a, w_pe_b, w_uk_t, tab_c, tab_s):
    m = z.shape[0]
    tm = MM_TM
    hp = N_HEADS // 2
    return pl.pallas_call(
        _q_proj_body,
        grid=(hp, m // tm),
        in_specs=[pl.BlockSpec((tm, Q_RANK), lambda h, i: (i, qa_col // Q_RANK)),
                  pl.BlockSpec((1, Q_RANK), lambda h, i: (0, 0)),
                  pl.BlockSpec((Q_RANK, 2 * QK_NOPE), lambda h, i: (0, h)),
                  pl.BlockSpec((Q_RANK, 2 * QK_ROPE), lambda h, i: (0, h)),
                  pl.BlockSpec((Q_RANK, 2 * QK_ROPE), lambda h, i: (0, h)),
                  pl.BlockSpec((2, QK_NOPE, KV_RANK), lambda h, i: (h, 0, 0)),
                  pl.BlockSpec((tm, LANES), lambda h, i: (i, 0)),
                  pl.BlockSpec((tm, LANES), lambda h, i: (i, 0))],
        out_specs=[pl.BlockSpec((tm, 2 * KV_RANK), lambda h, i: (i, h)),
                   pl.BlockSpec((tm, 2 * LANES), lambda h, i: (i, h))],
        out_shape=[jax.ShapeDtypeStruct((m, N_HEADS * KV_RANK), BF16),
                   jax.ShapeDtypeStruct((m, N_HEADS * LANES), BF16)],
        compiler_params=_cparams(("arbitrary", "arbitrary")),
        name="q_proj",
    )(z, q_norm_g.reshape(1, Q_RANK), w_nope, w_pe_a, w_pe_b, w_uk_t, tab_c, tab_s)


def _kv_body(kv_ref, kp_ref, g_ref, tk_ref, ckv_ref, kpe_ref, ckvb_ref, kpeb_ref):
    x = kv_ref[...]
    c = x * lax.rsqrt(jnp.mean(x * x, axis=-1, keepdims=True) + EPS) * g_ref[...]
    ckv_ref[...] = c
    ckvb_ref[...] = c.astype(BF16)
    t = kp_ref[...] * tk_ref[...]
    r = t + pltpu.roll(t, QK_ROPE, 1)
    kpe_ref[...] = r[:, :QK_ROPE]
    lane = lax.broadcasted_iota(jnp.int32, r.shape, 1)
    kpeb_ref[...] = jnp.where(lane < QK_ROPE, r, 0.0).astype(BF16)


def _kv_proj(z, kv_col, kp_col, kv_norm_g, tab_k):
    m = z.shape[0]
    tm = MM_TM
    row = lambda width: pl.BlockSpec((tm, width), lambda i: (i, 0))
    return pl.pallas_call(
        _kv_body,
        grid=(m // tm,),
        in_specs=[pl.BlockSpec((tm, KV_RANK), lambda i: (i, kv_col // KV_RANK)),
                  pl.BlockSpec((tm, LANES), lambda i: (i, kp_col // LANES)),
                  pl.BlockSpec((1, KV_RANK), lambda i: (0, 0)),
                  row(LANES)],
        out_specs=[row(KV_RANK), row(QK_ROPE), row(KV_RANK), row(LANES)],
        out_shape=[jax.ShapeDtypeStruct((m, KV_RANK), F32),
                   jax.ShapeDtypeStruct((m, QK_ROPE), F32),
                   jax.ShapeDtypeStruct((m, KV_RANK), BF16),
                   jax.ShapeDtypeStruct((m, LANES), BF16)],
        compiler_params=_cparams(("arbitrary",)),
        name="kv_proj",
    )(z, z, kv_norm_g.reshape(1, KV_RANK), tab_k)


def _uv_body(ol_ref, w_ref, o_ref):
    o_ref[:, :V_DIM] = _dot(ol_ref[:, :KV_RANK], w_ref[0]).astype(o_ref.dtype)
    o_ref[:, V_DIM:] = _dot(ol_ref[:, KV_RANK:], w_ref[1]).astype(o_ref.dtype)


def _uv_proj(o_lat, w_uv_t):
    m = o_lat.shape[0]
    tm = MM_TM
    return pl.pallas_call(
        _uv_body,
        grid=(N_HEADS // 2, m // tm),
        in_specs=[pl.BlockSpec((tm, 2 * KV_RANK), lambda h, i: (i, h)),
                  pl.BlockSpec((2, KV_RANK, V_DIM), lambda h, i: (h, 0, 0))],
        out_specs=pl.BlockSpec((tm, 2 * V_DIM), lambda h, i: (i, h)),
        out_shape=jax.ShapeDtypeStruct((m, N_HEADS * V_DIM), BF16),
        compiler_params=_cparams(("arbitrary", "arbitrary")),
        name="uv_proj",
    )(o_lat, w_uv_t)


def _softmax_update(s, v, m_s, l_s, acc_s):
    m_old = m_s[...]
    m_new = jnp.maximum(m_old, jnp.max(s, axis=-1, keepdims=True))
    alpha = jnp.exp(m_old - m_new)
    p = jnp.exp(s - m_new)
    l_s[...] = alpha * l_s[...] + jnp.sum(p, axis=-1, keepdims=True)
    acc_s[...] = alpha * acc_s[...] + _dot(p.astype(BF16), v)
    m_s[...] = m_new


def _attn_prompt_body(ql_ref, qp_ref, ck_ref, kp_ref, o_ref, m_s, l_s, acc_s, *, scale):
    i = pl.program_id(1)
    j = pl.program_id(2)
    nj = pl.num_programs(2)

    @pl.when(j == 0)
    def _():
        m_s[...] = jnp.full_like(m_s, -jnp.inf)
        l_s[...] = jnp.zeros_like(l_s)
        acc_s[...] = jnp.zeros_like(acc_s)

    @pl.when(j * ATT_TK <= i * ATT_TQ + (ATT_TQ - 1))
    def _():
        ck = ck_ref[...]
        s = (_dot_nt(ql_ref[...], ck) + _dot_nt(qp_ref[...], kp_ref[...])) * scale
        q_tok = i * ATT_TQ + lax.broadcasted_iota(jnp.int32, s.shape, 0) // N_HEADS
        k_tok = j * ATT_TK + lax.broadcasted_iota(jnp.int32, s.shape, 1)
        s = jnp.where(k_tok <= q_tok, s, MASK_VALUE)
        _softmax_update(s, ck, m_s, l_s, acc_s)

    @pl.when(j == nj - 1)
    def _():
        o_ref[...] = (acc_s[...] / l_s[...]).astype(o_ref.dtype)


def _attn_prompt(ql, qp, ckv_b, kpe_b, n_batch, seq, scale):
    rows = ATT_TQ * N_HEADS
    nq = seq // ATT_TQ
    nk = seq // ATT_TK

    def kmap(b, i, j):
        jmax = (i * ATT_TQ + (ATT_TQ - 1)) // ATT_TK
        return (b * nk + jnp.minimum(j, jmax), 0)

    return pl.pallas_call(
        functools.partial(_attn_prompt_body, scale=scale),
        grid=(n_batch, nq, nk),
        in_specs=[pl.BlockSpec((rows, KV_RANK), lambda b, i, j: (b * nq + i, 0)),
                  pl.BlockSpec((rows, LANES), lambda b, i, j: (b * nq + i, 0)),
                  pl.BlockSpec((ATT_TK, KV_RANK), kmap),
                  pl.BlockSpec((ATT_TK, LANES), kmap)],
        out_specs=pl.BlockSpec((rows, KV_RANK), lambda b, i, j: (b * nq + i, 0)),
        out_shape=jax.ShapeDtypeStruct((n_batch * seq * N_HEADS, KV_RANK), BF16),
        scratch_shapes=[pltpu.VMEM((rows, 1), F32), pltpu.VMEM((rows, 1), F32),
                        pltpu.VMEM((rows, KV_RANK), F32)],
        compiler_params=_cparams(("arbitrary", "arbitrary", "arbitrary")),
        name="attn_prompt",
    )(ql, qp, ckv_b, kpe_b)


def _attn_decode_body(pt_ref, ql_ref, qp_ref, cn_ref, kn_ref, *rest, scale, n_new):
    pages_c = rest[:DEC_PAGES]
    pages_k = rest[DEC_PAGES:2 * DEC_PAGES]
    o_ref, m_s, l_s, acc_s = rest[2 * DEC_PAGES:]
    j = pl.program_id(1)
    nj = pl.num_programs(1)

    @pl.when(j == 0)
    def _():
        m_s[...] = jnp.full_like(m_s, -jnp.inf)
        l_s[...] = jnp.zeros_like(l_s)
        acc_s[...] = jnp.zeros_like(acc_s)

    ql = ql_ref[...]
    qp = qp_ref[...]
    ck = jnp.concatenate([p[...].astype(BF16) for p in pages_c], axis=0)
    kp = jnp.concatenate([p[...].astype(BF16) for p in pages_k], axis=0)
    s = (_dot_nt(ql, ck) + _dot_nt(qp[:, :QK_ROPE], kp)) * scale
    _softmax_update(s, ck, m_s, l_s, acc_s)

    @pl.when(j == nj - 1)
    def _():
        cn = cn_ref[...]
        s2 = (_dot_nt(ql, cn) + _dot_nt(qp, kn_ref[...])) * scale
        q_tok = lax.broadcasted_iota(jnp.int32, s2.shape, 0) // N_HEADS
        k_tok = lax.broadcasted_iota(jnp.int32, s2.shape, 1)
        s2 = jnp.where(k_tok <= q_tok, s2, MASK_VALUE)
        _softmax_update(s2, cn, m_s, l_s, acc_s)
        o_ref[...] = (acc_s[...] / l_s[...]).astype(o_ref.dtype)


def _attn_decode(page_table, ql, qp, ckv_new, kpe_new, cache_ckv, cache_kpe, scale, n_new):
    n_seq, n_pages = page_table.shape
    rows = ql.shape[1]
    steps = n_pages // DEC_PAGES

    def page_spec(width, p):
        return pl.BlockSpec((None, PAGE_SIZE, width), lambda b, j, pt: (pt[b, j * DEC_PAGES + p], 0, 0))

    per_seq = lambda width, r: pl.BlockSpec((None, r, width), lambda b, j, pt: (b, 0, 0))
    grid_spec = pltpu.PrefetchScalarGridSpec(
        num_scalar_prefetch=1,
        grid=(n_seq, steps),
        in_specs=[per_seq(KV_RANK, rows), per_seq(LANES, rows),
                  per_seq(KV_RANK, SUBLANES), per_seq(LANES, SUBLANES)]
                 + [page_spec(KV_RANK, p) for p in range(DEC_PAGES)]
                 + [page_spec(QK_ROPE, p) for p in range(DEC_PAGES)],
        out_specs=per_seq(KV_RANK, rows),
        scratch_shapes=[pltpu.VMEM((rows, 1), F32), pltpu.VMEM((rows, 1), F32),
                        pltpu.VMEM((rows, KV_RANK), F32)])
    return pl.pallas_call(
        functools.partial(_attn_decode_body, scale=scale, n_new=n_new),
        grid_spec=grid_spec,
        out_shape=jax.ShapeDtypeStruct((n_seq, rows, KV_RANK), BF16),
        compiler_params=_cparams(("arbitrary", "arbitrary")),
        name="attn_decode",
    )(page_table, ql, qp, ckv_new, kpe_new, *([cache_ckv] * DEC_PAGES), *([cache_kpe] * DEC_PAGES))


def _router_body(h_ref, w_ref, b_ref, idx_ref, wt_ref):
    h = h_ref[...]
    w = w_ref[...]
    h_hi = h.astype(BF16)
    h_lo = (h - h_hi.astype(F32)).astype(BF16)
    w_hi = w.astype(BF16)
    w_lo = (w - w_hi.astype(F32)).astype(BF16)
    logits = _dot(h_hi, w_hi) + (_dot(h_hi, w_lo) + _dot(h_lo, w_hi)) + b_ref[...]
    lane = lax.broadcasted_iota(jnp.int32, logits.shape, 1).astype(F32)
    vals = jnp.where(lane < N_EXPERTS, logits, -jnp.inf)
    idx_out = jnp.zeros(logits.shape, F32)
    wt_out = jnp.zeros(logits.shape, F32)
    top = None
    den = jnp.zeros((logits.shape[0], 1), F32)
    for k in range(TOP_K):
        m = jnp.max(vals, axis=-1, keepdims=True)
        sel = jnp.min(jnp.where(vals == m, lane, float(LANES)), axis=-1, keepdims=True)
        if top is None:
            top = m
        e = jnp.exp(m - top)
        den = den + e
        idx_out = jnp.where(lane == k, sel, idx_out)
        wt_out = jnp.where(lane == k, e, wt_out)
        vals = jnp.where(lane == sel, -jnp.inf, vals)
    idx_ref[...] = idx_out.astype(jnp.int32)
    wt_ref[...] = wt_out / den


def _router(h2, w_router_p, b_router_p):
    m, d = h2.shape
    tm = MM_TM
    return pl.pallas_call(
        _router_body,
        grid=(m // tm,),
        in_specs=[pl.BlockSpec((tm, d), lambda i: (i, 0)),
                  pl.BlockSpec((d, LANES), lambda i: (0, 0)),
                  pl.BlockSpec((1, LANES), lambda i: (0, 0))],
        out_specs=[pl.BlockSpec((tm, LANES), lambda i: (i, 0)),
                   pl.BlockSpec((tm, LANES), lambda i: (i, 0))],
        out_shape=[jax.ShapeDtypeStruct((m, LANES), jnp.int32),
                   jax.ShapeDtypeStruct((m, LANES), F32)],
        compiler_params=_cparams(("arbitrary",)),
        name="router",
    )(h2, w_router_p, b_router_p)


def _moe_gu_body(te_ref, tv_ref, x_ref, wg_ref, wu_ref, bg_ref, bu_ref, o_ref):
    t = pl.program_id(1)

    @pl.when(tv_ref[t] != 0)
    def _():
        x = x_ref[...]
        g = _dot(x, wg_ref[...].astype(BF16)) + bg_ref[...]
        u = _dot(x, wu_ref[...].astype(BF16)) + bu_ref[...]
        gate = jnp.minimum(g, SWIGLU_LIMIT)
        up = jnp.clip(u, -SWIGLU_LIMIT, SWIGLU_LIMIT)
        o_ref[...] = ((up + 1.0) * gate * _sigmoid(SWIGLU_ALPHA * gate)).astype(o_ref.dtype)

    @pl.when(tv_ref[t] == 0)
    def _():
        o_ref[...] = jnp.zeros_like(o_ref)


def _moe_gu(tile_e, tile_v, xs, w_gu, b_gu):
    rows, d = xs.shape
    f = w_gu.shape[2] // 2
    nb = f // MOE_TN
    grid_spec = pltpu.PrefetchScalarGridSpec(
        num_scalar_prefetch=2,
        grid=(nb, rows // MOE_TM),
        in_specs=[pl.BlockSpec((MOE_TM, d), lambda n, t, te, tv: (t, 0)),
                  pl.BlockSpec((None, d, MOE_TN), lambda n, t, te, tv: (te[t], 0, n)),
                  pl.BlockSpec((None, d, MOE_TN), lambda n, t, te, tv: (te[t], 0, n + nb)),
                  pl.BlockSpec((None, 1, MOE_TN), lambda n, t, te, tv: (te[t], 0, n)),
                  pl.BlockSpec((None, 1, MOE_TN), lambda n, t, te, tv: (te[t], 0, n + nb))],
        out_specs=pl.BlockSpec((MOE_TM, MOE_TN), lambda n, t, te, tv: (t, n)))
    return pl.pallas_call(
        _moe_gu_body,
        grid_spec=grid_spec,
        out_shape=jax.ShapeDtypeStruct((rows, f), BF16),
        compiler_params=_cparams(("arbitrary", "arbitrary")),
        name="moe_gate_up",
    )(tile_e, tile_v, xs, w_gu, w_gu, b_gu, b_gu)


def _moe_down_body(te_ref, tv_ref, a_ref, w_ref, b_ref, rw_ref, o_ref):
    t = pl.program_id(1)

    @pl.when(tv_ref[t] != 0)
    def _():
        o_ref[...] = (_dot(a_ref[...], w_ref[...].astype(BF16)) + b_ref[...]) * rw_ref[...]

    @pl.when(tv_ref[t] == 0)
    def _():
        o_ref[...] = jnp.zeros_like(o_ref)


def _moe_down(tile_e, tile_v, act, w_down, b_down, row_w):
    rows, f = act.shape
    d = w_down.shape[2]
    nb = d // MOE_TN
    grid_spec = pltpu.PrefetchScalarGridSpec(
        num_scalar_prefetch=2,
        grid=(nb, rows // MOE_TM),
        in_specs=[pl.BlockSpec((MOE_TM, f), lambda n, t, te, tv: (t, 0)),
                  pl.BlockSpec((None, f, MOE_TN), lambda n, t, te, tv: (te[t], 0, n)),
                  pl.BlockSpec((None, 1, MOE_TN), lambda n, t, te, tv: (te[t], 0, n)),
                  pl.BlockSpec((MOE_TM, 1), lambda n, t, te, tv: (t, 0))],
        out_specs=pl.BlockSpec((MOE_TM, MOE_TN), lambda n, t, te, tv: (t, n)))
    return pl.pallas_call(
        _moe_down_body,
        grid_spec=grid_spec,
        out_shape=jax.ShapeDtypeStruct((rows, d), F32),
        compiler_params=_cparams(("arbitrary", "arbitrary")),
        name="moe_down",
    )(tile_e, tile_v, act, w_down, b_down, row_w)


def _moe(h2, w_router, b_router, w_gu, b_gu, w_down, b_down):
    n, d = h2.shape
    w_router_p = jnp.pad(w_router, ((0, 0), (0, LANES - N_EXPERTS)))
    b_router_p = jnp.pad(b_router, (0, LANES - N_EXPERTS)).reshape(1, LANES)
    idx_p, wt_p = _router(h2, w_router_p, b_router_p)
    top_i = idx_p[:, :TOP_K]
    wts = wt_p[:, :TOP_K]

    n_rows = n * TOP_K
    rows_pad = n_rows + N_EXPERTS * MOE_TM
    n_tiles = rows_pad // MOE_TM
    flat_e = top_i.reshape(-1)
    order = jnp.argsort(flat_e)
    e_sorted = flat_e[order]
    sizes = jnp.bincount(flat_e, length=N_EXPERTS).astype(jnp.int32)
    psizes = ((sizes + MOE_TM - 1) // MOE_TM) * MOE_TM
    pend = jnp.cumsum(psizes)
    pstart = pend - psizes
    start = jnp.cumsum(sizes) - sizes
    dest = pstart[e_sorted] + (jnp.arange(n_rows, dtype=jnp.int32) - start[e_sorted])
    tok_pad = jnp.zeros((rows_pad,), jnp.int32).at[dest].set((order // TOP_K).astype(jnp.int32))
    w_pad = jnp.zeros((rows_pad,), F32).at[dest].set(wts.reshape(-1)[order])
    pos = jnp.zeros((n_rows,), jnp.int32).at[order].set(dest).reshape(n, TOP_K)
    tile_start = jnp.arange(n_tiles, dtype=jnp.int32) * MOE_TM
    tile_e = jnp.minimum(jnp.searchsorted(pend, tile_start, side='right'), N_EXPERTS - 1).astype(jnp.int32)
    tile_v = (tile_start < pend[-1]).astype(jnp.int32)

    xs = h2.astype(BF16)[tok_pad]
    act = _moe_gu(tile_e, tile_v, xs, w_gu, b_gu.reshape(N_EXPERTS, 1, -1))
    eo = _moe_down(tile_e, tile_v, act, w_down, b_down.reshape(N_EXPERTS, 1, -1), w_pad.reshape(rows_pad, 1))
    return eo[pos].sum(axis=1)


def kernel(x_prompt, x_sample, c_prompt, c_sample, cache_ckv, cache_kpe, page_table, state_ssm_re, state_ssm_im, w_ada, b_ada, g_mix, w_in, ssm_lam_re, ssm_lam_im, ssm_log_dt, ssm_b_re, ssm_b_im, ssm_c_re, ssm_c_im, ssm_d, w_glu, q_norm_g, kv_norm_g, w_qb, w_uk, w_uv, w_bo, w_o, g_ffn, w_router, b_router, w_gu, b_gu, w_down, b_down, g_final):
    n_batch, seq, d = x_prompt.shape
    n_seq, n_new, _ = x_sample.shape
    depth = w_ada.shape[0]
    d_ssm = ssm_d.shape[1]
    n_groups = d_ssm // SSM_GROUP
    assert depth == 1 and n_seq == ROW_TILE and seq % SSM_CHUNK == 0 and seq % MM_TM == 0
    assert n_new <= SUBLANES and page_table.shape[1] % DEC_PAGES == 0
    np_rows = n_batch * seq
    ns_rows = n_seq * n_new
    n_rows = np_rows + ns_rows
    assert n_rows % MM_TM == 0 and ns_rows == MM_TM
    tiles_per_batch = seq // ROW_TILE
    past_len = page_table.shape[1] * PAGE_SIZE
    scale = float((QK_NOPE + QK_ROPE) ** -0.5)

    x_all = jnp.concatenate([x_prompt.reshape(np_rows, d),
                             x_sample.transpose(1, 0, 2).reshape(ns_rows, d)], axis=0)
    c_all = jnp.concatenate([c_prompt, c_sample], axis=0)

    w_ada_b = w_ada[0].astype(BF16)
    sp = (d_ssm, d_ssm + Q_RANK, d_ssm + Q_RANK + KV_RANK, d_ssm + Q_RANK + KV_RANK + QK_ROPE)
    half = QK_ROPE // 2
    swap = jnp.concatenate([jnp.arange(half, QK_ROPE), jnp.arange(0, half)])
    w_in0 = w_in[0]
    kp_col = sp[2]
    ga_col = kp_col + 2 * LANES
    gb_col = ga_col + d
    w_in_r = jnp.concatenate([w_in0[:, :sp[3]], w_in0[:, sp[2] + swap],
                              jnp.zeros((d, ga_col - sp[3] - QK_ROPE), F32), w_in0[:, sp[3]:]], axis=1).astype(BF16)
    assert ga_col % 512 == 0 and w_in_r.shape[1] % 512 == 0
    w_qb_h = w_qb[0].reshape(Q_RANK, N_HEADS, QK_NOPE + QK_ROPE)
    w_nope = w_qb_h[:, :, :QK_NOPE].reshape(Q_RANK, N_HEADS * QK_NOPE).astype(BF16)
    w_pe_a = w_qb_h[:, :, QK_NOPE:].reshape(Q_RANK, N_HEADS * QK_ROPE).astype(BF16)
    w_pe_b = w_qb_h[:, :, QK_NOPE + swap].reshape(Q_RANK, N_HEADS * QK_ROPE).astype(BF16)
    w_uk_t = w_uk[0].transpose(1, 2, 0).astype(BF16)
    w_uv_t = w_uv[0].transpose(1, 0, 2).astype(BF16)
    w_glu_b = w_glu[0].astype(BF16)
    w_bo_b = w_bo[0].astype(BF16)
    w_o_b = w_o[0].astype(BF16)
    wb, wc, lam8, lamr8, pw = _ssm_params(ssm_lam_re[0], ssm_lam_im[0], ssm_log_dt[0], ssm_b_re[0], ssm_b_im[0],
                                          ssm_c_re[0], ssm_c_im[0])
    d_skip = ssm_d[0].reshape(1, d_ssm)

    inv = ROPE_THETA ** (-jnp.arange(half, dtype=F32) / half)
    pos = jnp.concatenate([jnp.tile(jnp.arange(seq, dtype=jnp.int32), n_batch),
                           jnp.repeat(past_len + jnp.arange(n_new, dtype=jnp.int32), n_seq)])
    ang = pos.astype(F32)[:, None] * inv
    cos, sin = jnp.cos(ang), jnp.sin(ang)
    tab_c = jnp.concatenate([cos, cos, cos, cos], axis=1)
    tab_s = jnp.concatenate([-sin, sin, -sin, sin], axis=1)
    tab_k = jnp.concatenate([cos, cos, -sin, sin], axis=1)

    mod = _ada(c_all, w_ada_b, b_ada[0])
    mod_all = jnp.concatenate([jnp.broadcast_to(mod[:n_batch, None, :], (n_batch, ROW_TILE, 6 * d)),
                               mod[n_batch:][None]], axis=0)

    h = _norm_mod(x_all, g_mix[0], mod_all, tiles_per_batch, n_batch)
    z = _matmul(h, w_in_r, tn=512, out_dtype=F32, name="in_proj")

    gy_p, st_p = _ssm_prompt(z, n_batch, seq, wb, wc, d_skip, lam8, lamr8, pw)
    h0_re = state_ssm_re[0].reshape(n_seq, n_groups * SSM_STATE)
    h0_im = state_ssm_im[0].reshape(n_seq, n_groups * SSM_STATE)
    gy_s, sre_s, sim_s = _ssm_sample(z, np_rows, n_new, n_seq, wb, wc, d_skip, lam8, h0_re, h0_im)
    gy = jnp.concatenate([gy_p, gy_s], axis=0)
    gated_a = _glu_branch(gy, w_glu_b, z, ga_col)

    ql, qp = _q_proj(z, sp[0], q_norm_g[0], w_nope, w_pe_a, w_pe_b, w_uk_t, tab_c, tab_s)
    ckv, kpe, ckv_b, kpe_b = _kv_proj(z, sp[1], kp_col, kv_norm_g[0], tab_k)

    ol_p = _attn_prompt(ql.reshape(n_rows * N_HEADS, KV_RANK), qp.reshape(n_rows * N_HEADS, LANES),
                        ckv_b, kpe_b, n_batch, seq, scale)

    def seq_major(x, width):
        return x[np_rows:].reshape(n_new, n_seq, -1, width).transpose(1, 0, 2, 3).reshape(n_seq, -1, width)

    def pad_keys(x):
        return jnp.pad(x, ((0, 0), (0, SUBLANES - n_new), (0, 0)))

    ol_s = _attn_decode(page_table, seq_major(ql, KV_RANK), seq_major(qp, LANES),
                        pad_keys(seq_major(ckv_b, KV_RANK)), pad_keys(seq_major(kpe_b, LANES)),
                        cache_ckv[0], cache_kpe[0], scale, n_new)
    ol_s = ol_s.reshape(n_seq, n_new, N_HEADS * KV_RANK).transpose(1, 0, 2).reshape(ns_rows, N_HEADS * KV_RANK)
    o_lat = jnp.concatenate([ol_p.reshape(np_rows, N_HEADS * KV_RANK), ol_s], axis=0)
    o = _uv_proj(o_lat, w_uv_t)
    mix = _bo_mix(o, w_bo_b, gated_a, z, gb_col)
    mixed = _matmul(mix, w_o_b, tn=512, out_dtype=F32, name="out_proj")

    x1, h2 = _resid_norm_mod(x_all, mixed, g_ffn[0], mod_all, tiles_per_batch, n_batch)
    moe_out = _moe(h2, w_router[0], b_router[0], w_gu[0], b_gu[0], w_down[0], b_down[0])
    y = _final(x1, moe_out, g_final, mod_all, tiles_per_batch, n_batch)

    def sample_rows(x):
        return x[np_rows:].reshape(n_new, n_seq, -1).transpose(1, 0, 2)

    w = n_groups * SSM_STATE // wb.shape[0]
    st_p = st_p.reshape(n_batch, wb.shape[0], 2, SSM_GB, SSM_STATE)
    sre_p = st_p[:, :, 0].reshape(1, n_batch, n_groups, SSM_STATE)
    sim_p = st_p[:, :, 1].reshape(1, n_batch, n_groups, SSM_STATE)
    return (y[:np_rows].reshape(n_batch, seq, d),
            sample_rows(y),
            ckv[:np_rows].reshape(1, n_batch, seq, KV_RANK),
            kpe[:np_rows].reshape(1, n_batch, seq, QK_ROPE),
            sre_p, sim_p,
            sample_rows(ckv)[None], sample_rows(kpe)[None],
            sre_s.reshape(1, n_seq, n_groups, SSM_STATE), sim_s.reshape(1, n_seq, n_groups, SSM_STATE))
```

```python
import functools
import math

import jax
import jax.numpy as jnp
from jax import lax
from jax.experimental import pallas as pl
from jax.experimental.pallas import tpu as pltpu

F32 = jnp.float32
BF16 = jnp.bfloat16

SSM_GROUP = 16
SSM_STATE = 64
N_HEADS = 16
QK_NOPE = 128
QK_ROPE = 64
V_DIM = 128
Q_RANK = 512
KV_RANK = 256
ROPE_THETA = 10000.0
PAGE_SIZE = 128
MASK_VALUE = -1e30
N_EXPERTS = 32
TOP_K = 4
SWIGLU_LIMIT = 7.0
SWIGLU_ALPHA = 1.702
EPS = 1e-6

LANES = 128
SUBLANES = 8
VMEM_LIMIT = 56 * 1024 * 1024

ROW_TILE = 128
MM_TM = 512
SSM_CHUNK = 256
SSM_SUB = SSM_CHUNK // SUBLANES
SSM_GB = 8
ATT_TQ = 128
ATT_TK = 256
DEC_PAGES = 32
QK_CAT = KV_RANK + LANES
MOE_TM = 256
MOE_TN = 512


def _cparams(sem):
    return pltpu.CompilerParams(dimension_semantics=sem, vmem_limit_bytes=VMEM_LIMIT)


def _sigmoid(x):
    return 1.0 / (1.0 + jnp.exp(-x))


def _gelu_tanh(x):
    return 0.5 * x * (1.0 + jnp.tanh(math.sqrt(2.0 / math.pi) * (x + 0.044715 * (x * x * x))))


def _dot(a, b):
    return jnp.dot(a, b, preferred_element_type=F32)


def _dot_nt(a, b):
    return lax.dot_general(a, b, (((1,), (1,)), ((), ())), preferred_element_type=F32)


def _ada_body(c_ref, w_ref, b_ref, o_ref):
    c = c_ref[...]
    a = (c * _sigmoid(c)).astype(BF16)
    o_ref[...] = _dot(a, w_ref[...]) + b_ref[...]


def _ada(c_all, w_ada_b, b_ada):
    m, d = c_all.shape
    n = w_ada_b.shape[1]
    tn = 1024
    return pl.pallas_call(
        _ada_body,
        grid=(n // tn,),
        in_specs=[pl.BlockSpec((m, d), lambda j: (0, 0)),
                  pl.BlockSpec((d, tn), lambda j: (0, j)),
                  pl.BlockSpec((1, tn), lambda j: (0, j))],
        out_specs=pl.BlockSpec((m, tn), lambda j: (0, j)),
        out_shape=jax.ShapeDtypeStruct((m, n), F32),
        compiler_params=_cparams(("arbitrary",)),
        name="ada_mod",
    )(c_all, w_ada_b, b_ada.reshape(1, n))


def _mod_spec(chunk, d, tiles_per_batch, n_batch):
    return pl.BlockSpec((None, ROW_TILE, d),
                        lambda i: (jnp.minimum(i // tiles_per_batch, n_batch), 0, chunk))


def _norm_mod_body(x_ref, g_ref, sh_ref, sc_ref, o_ref):
    x = x_ref[...]
    y = x * lax.rsqrt(jnp.mean(x * x, axis=-1, keepdims=True) + EPS)
    o_ref[...] = ((y * g_ref[...]) * (1.0 + sc_ref[...]) + sh_ref[...]).astype(o_ref.dtype)


def _norm_mod(x, g, mod_all, tiles_per_batch, n_batch):
    n, d = x.shape
    return pl.pallas_call(
        _norm_mod_body,
        grid=(n // ROW_TILE,),
        in_specs=[pl.BlockSpec((ROW_TILE, d), lambda i: (i, 0)),
                  pl.BlockSpec((1, d), lambda i: (0, 0)),
                  _mod_spec(0, d, tiles_per_batch, n_batch),
                  _mod_spec(1, d, tiles_per_batch, n_batch)],
        out_specs=pl.BlockSpec((ROW_TILE, d), lambda i: (i, 0)),
        out_shape=jax.ShapeDtypeStruct((n, d), BF16),
        compiler_params=_cparams(("arbitrary",)),
        name="norm_mod1",
    )(x, g.reshape(1, d), mod_all, mod_all)


def _resid_norm_mod_body(x_ref, m_ref, g_ref, gt_ref, sh_ref, sc_ref, x1_ref, h_ref):
    x = x_ref[...] + gt_ref[...] * m_ref[...]
    x1_ref[...] = x
    y = x * lax.rsqrt(jnp.mean(x * x, axis=-1, keepdims=True) + EPS)
    h_ref[...] = (y * g_ref[...]) * (1.0 + sc_ref[...]) + sh_ref[...]


def _resid_norm_mod(x, mixed, g, mod_all, tiles_per_batch, n_batch):
    n, d = x.shape
    row = pl.BlockSpec((ROW_TILE, d), lambda i: (i, 0))
    return pl.pallas_call(
        _resid_norm_mod_body,
        grid=(n // ROW_TILE,),
        in_specs=[row, row, pl.BlockSpec((1, d), lambda i: (0, 0)),
                  _mod_spec(2, d, tiles_per_batch, n_batch),
                  _mod_spec(3, d, tiles_per_batch, n_batch),
                  _mod_spec(4, d, tiles_per_batch, n_batch)],
        out_specs=[row, row],
        out_shape=[jax.ShapeDtypeStruct((n, d), F32), jax.ShapeDtypeStruct((n, d), F32)],
        compiler_params=_cparams(("arbitrary",)),
        name="resid_norm_mod2",
    )(x, mixed, g.reshape(1, d), mod_all, mod_all, mod_all)


def _final_body(x_ref, m_ref, g_ref, gt_ref, o_ref):
    x = x_ref[...] + gt_ref[...] * m_ref[...]
    y = x * lax.rsqrt(jnp.mean(x * x, axis=-1, keepdims=True) + EPS)
    o_ref[...] = y * g_ref[...]


def _final(x1, moe_out, g, mod_all, tiles_per_batch, n_batch):
    n, d = x1.shape
    row = pl.BlockSpec((ROW_TILE, d), lambda i: (i, 0))
    return pl.pallas_call(
        _final_body,
        grid=(n // ROW_TILE,),
        in_specs=[row, row, pl.BlockSpec((1, d), lambda i: (0, 0)),
                  _mod_spec(5, d, tiles_per_batch, n_batch)],
        out_specs=row,
        out_shape=jax.ShapeDtypeStruct((n, d), F32),
        compiler_params=_cparams(("arbitrary",)),
        name="final_norm",
    )(x1, moe_out, g.reshape(1, d), mod_all)


def _mm_body(a_ref, w_ref, o_ref):
    o_ref[...] = _dot(a_ref[...].astype(BF16), w_ref[...]).astype(o_ref.dtype)


def _matmul(a, w, *, tn, out_dtype, name):
    m, k = a.shape
    n = w.shape[1]
    tm = MM_TM
    return pl.pallas_call(
        _mm_body,
        grid=(n // tn, m // tm),
        in_specs=[pl.BlockSpec((tm, k), lambda j, i: (i, 0)),
                  pl.BlockSpec((k, tn), lambda j, i: (0, j))],
        out_specs=pl.BlockSpec((tm, tn), lambda j, i: (i, j)),
        out_shape=jax.ShapeDtypeStruct((m, n), out_dtype),
        compiler_params=_cparams(("arbitrary", "arbitrary")),
        name=name,
    )(a, w)


def _glu_body(a_ref, w1_ref, w2_ref, ga_ref, o_ref):
    a = a_ref[...].astype(BF16)
    g1 = _dot(a, w1_ref[...])
    g2 = _dot(a, w2_ref[...])
    o_ref[...] = _sigmoid(ga_ref[...]) * (g1 * _sigmoid(g2))


def _glu_branch(gy, w_glu_b, z, ga_col):
    m, k = gy.shape
    d = w_glu_b.shape[1] // 2
    tm, tn = MM_TM, 512
    nb = d // tn
    return pl.pallas_call(
        _glu_body,
        grid=(nb, m // tm),
        in_specs=[pl.BlockSpec((tm, k), lambda j, i: (i, 0)),
                  pl.BlockSpec((k, tn), lambda j, i: (0, j)),
                  pl.BlockSpec((k, tn), lambda j, i: (0, j + nb)),
                  pl.BlockSpec((tm, tn), lambda j, i: (i, j + ga_col // tn))],
        out_specs=pl.BlockSpec((tm, tn), lambda j, i: (i, j)),
        out_shape=jax.ShapeDtypeStruct((m, d), F32),
        compiler_params=_cparams(("arbitrary", "arbitrary")),
        name="glu_branch",
    )(gy, w_glu_b, w_glu_b, z)


def _bo_mix_body(o_ref_in, w_ref, a_ref, gb_ref, out_ref):
    br_b = _dot(o_ref_in[...], w_ref[...])
    out_ref[...] = (a_ref[...] + _sigmoid(gb_ref[...]) * br_b).astype(out_ref.dtype)


def _bo_mix(o, w_bo_b, gated_a, z, gb_col):
    m, k = o.shape
    d = w_bo_b.shape[1]
    tm, tn = MM_TM, 512
    return pl.pallas_call(
        _bo_mix_body,
        grid=(d // tn, m // tm),
        in_specs=[pl.BlockSpec((tm, k), lambda j, i: (i, 0)),
                  pl.BlockSpec((k, tn), lambda j, i: (0, j)),
                  pl.BlockSpec((tm, tn), lambda j, i: (i, j)),
                  pl.BlockSpec((tm, tn), lambda j, i: (i, j + gb_col // tn))],
        out_specs=pl.BlockSpec((tm, tn), lambda j, i: (i, j)),
        out_shape=jax.ShapeDtypeStruct((m, d), BF16),
        compiler_params=_cparams(("arbitrary", "arbitrary")),
        name="bo_mix",
    )(o, w_bo_b, gated_a, z)


def _ssm_prompt_body(u_ref, wb_ref, wc_ref, d_ref, lam_ref, lamr_ref, pw_ref,
                     y_ref, st_ref, bu_s, hb_s, up_s, yp_s, hin_s, carry_s):
    w = lam_ref.shape[1] // 2
    ch = pl.program_id(2)

    @pl.when(ch == 0)
    def _():
        carry_s[...] = jnp.zeros_like(carry_s)

    for r in range(SSM_SUB):
        up_s[pl.ds(r * SUBLANES, SUBLANES), :] = u_ref[pl.ds(r, SUBLANES, stride=SSM_SUB), :]
    up = up_s[...]
    bu_s[...] = _dot(up.astype(BF16), wb_ref[...])

    lam_re = lam_ref[:, :w]
    lam_im = lam_ref[:, w:]

    def step(r, s):
        s_re, s_im = s
        row = pl.multiple_of(r * SUBLANES, SUBLANES)
        n_re = lam_re * s_re - lam_im * s_im + bu_s[pl.ds(row, SUBLANES), pl.ds(0, w)]
        n_im = lam_re * s_im + lam_im * s_re + bu_s[pl.ds(row, SUBLANES), pl.ds(w, w)]
        bu_s[pl.ds(row, SUBLANES), pl.ds(0, w)] = n_re
        bu_s[pl.ds(row, SUBLANES), pl.ds(w, w)] = n_im
        return n_re, n_im

    zero = jnp.zeros((SUBLANES, w), F32)
    e_re, e_im = lax.fori_loop(0, SSM_SUB, step, (zero, zero))

    lr_re = lamr_ref[0:1, :w]
    lr_im = lamr_ref[0:1, w:]
    h_re = carry_s[0:1, :w]
    h_im = carry_s[0:1, w:]
    for c in range(SUBLANES):
        hin_s[c:c + 1, :w] = h_re
        hin_s[c:c + 1, w:] = h_im
        n_re = lr_re * h_re - lr_im * h_im + e_re[c:c + 1, :]
        n_im = lr_re * h_im + lr_im * h_re + e_im[c:c + 1, :]
        h_re, h_im = n_re, n_im
    carry_s[0:1, :w] = h_re
    carry_s[0:1, w:] = h_im
    st_ref[0:1, :w] = h_re
    st_ref[0:1, w:] = h_im

    hin_re = hin_s[:, :w]
    hin_im = hin_s[:, w:]

    def fix(r, _):
        row = pl.multiple_of(r * SUBLANES, SUBLANES)
        p_re = pw_ref[pl.ds(row, SUBLANES), pl.ds(0, w)]
        p_im = pw_ref[pl.ds(row, SUBLANES), pl.ds(w, w)]
        f_re = bu_s[pl.ds(row, SUBLANES), pl.ds(0, w)] + (p_re * hin_re - p_im * hin_im)
        f_im = bu_s[pl.ds(row, SUBLANES), pl.ds(w, w)] + (p_re * hin_im + p_im * hin_re)
        hb_s[pl.ds(row, SUBLANES), pl.ds(0, w)] = f_re
        hb_s[pl.ds(row, SUBLANES), pl.ds(w, w)] = f_im
        return 0

    lax.fori_loop(0, SSM_SUB, fix, 0)

    y = _dot(hb_s[...].astype(BF16), wc_ref[...]) + d_ref[...] * up
    yp_s[...] = _gelu_tanh(y)
    for r in range(SSM_SUB):
        y_ref[pl.ds(r, SUBLANES, stride=SSM_SUB), :] = yp_s[pl.ds(r * SUBLANES, SUBLANES), :]


def _ssm_prompt(z, n_batch, seq, wb, wc, d_skip, lam8, lamr8, pw):
    nblk, cin, w2 = wb.shape
    n_chunks = seq // SSM_CHUNK
    grid = (n_batch, nblk, n_chunks)
    return pl.pallas_call(
        _ssm_prompt_body,
        grid=grid,
        in_specs=[pl.BlockSpec((SSM_CHUNK, cin), lambda b, k, c: (b * n_chunks + c, k)),
                  pl.BlockSpec((None, cin, w2), lambda b, k, c: (k, 0, 0)),
                  pl.BlockSpec((None, w2, cin), lambda b, k, c: (k, 0, 0)),
                  pl.BlockSpec((1, cin), lambda b, k, c: (0, k)),
                  pl.BlockSpec((None, SUBLANES, w2), lambda b, k, c: (k, 0, 0)),
                  pl.BlockSpec((None, SUBLANES, w2), lambda b, k, c: (k, 0, 0)),
                  pl.BlockSpec((None, SSM_CHUNK, w2), lambda b, k, c: (k, 0, 0))],
        out_specs=[pl.BlockSpec((SSM_CHUNK, cin), lambda b, k, c: (b * n_chunks + c, k)),
                   pl.BlockSpec((None, None, 1, w2), lambda b, k, c: (b, k, 0, 0))],
        out_shape=[jax.ShapeDtypeStruct((n_batch * seq, nblk * cin), F32),
                   jax.ShapeDtypeStruct((n_batch, nblk, 1, w2), F32)],
        scratch_shapes=[pltpu.VMEM((SSM_CHUNK, w2), F32),
                        pltpu.VMEM((SSM_CHUNK, w2), F32),
                        pltpu.VMEM((SSM_CHUNK, cin), F32),
                        pltpu.VMEM((SSM_CHUNK, cin), F32),
                        pltpu.VMEM((SUBLANES, w2), F32),
                        pltpu.VMEM((SUBLANES, w2), F32)],
        compiler_params=_cparams(("arbitrary", "arbitrary", "arbitrary")),
        name="ssm_prompt",
    )(z, wb, wc, d_skip, lam8, lamr8, pw)


def _ssm_sample_body(u_ref, wb_ref, wc_ref, d_ref, lam_ref, hre_ref, him_ref,
                     y_ref, ore_ref, oim_ref, bu_s, *, n_tok, n_seq):
    w = lam_ref.shape[1] // 2
    u = u_ref[...]
    bu_s[...] = _dot(u.astype(BF16), wb_ref[...])
    lam_re = lam_ref[0:1, :w]
    lam_im = lam_ref[0:1, w:]
    s_re = hre_ref[...]
    s_im = him_ref[...]
    for t in range(n_tok):
        rows = pl.ds(t * n_seq, n_seq)
        n_re = lam_re * s_re - lam_im * s_im + bu_s[rows, pl.ds(0, w)]
        n_im = lam_re * s_im + lam_im * s_re + bu_s[rows, pl.ds(w, w)]
        bu_s[rows, pl.ds(0, w)] = n_re
        bu_s[rows, pl.ds(w, w)] = n_im
        s_re, s_im = n_re, n_im
    ore_ref[...] = s_re
    oim_ref[...] = s_im
    y = _dot(bu_s[...].astype(BF16), wc_ref[...]) + d_ref[...] * u
    y_ref[...] = _gelu_tanh(y)


def _ssm_sample(z, row0, n_tok, n_seq, wb, wc, d_skip, lam8, h0_re, h0_im):
    nblk, cin, w2 = wb.shape
    w = w2 // 2
    rows = n_tok * n_seq
    rb = row0 // rows
    st = pl.BlockSpec((n_seq, w), lambda k: (0, k))
    return pl.pallas_call(
        functools.partial(_ssm_sample_body, n_tok=n_tok, n_seq=n_seq),
        grid=(nblk,),
        in_specs=[pl.BlockSpec((rows, cin), lambda k: (rb, k)),
                  pl.BlockSpec((None, cin, w2), lambda k: (k, 0, 0)),
                  pl.BlockSpec((None, w2, cin), lambda k: (k, 0, 0)),
                  pl.BlockSpec((1, cin), lambda k: (0, k)),
                  pl.BlockSpec((None, SUBLANES, w2), lambda k: (k, 0, 0)),
                  st, st],
        out_specs=[pl.BlockSpec((rows, cin), lambda k: (0, k)), st, st],
        out_shape=[jax.ShapeDtypeStruct((rows, nblk * cin), F32),
                   jax.ShapeDtypeStruct((n_seq, nblk * w), F32),
                   jax.ShapeDtypeStruct((n_seq, nblk * w), F32)],
        scratch_shapes=[pltpu.VMEM((rows, w2), F32)],
        compiler_params=_cparams(("arbitrary",)),
        name="ssm_sample",
    )(z, wb, wc, d_skip, lam8, h0_re, h0_im)


def _ssm_params(lam_re, lam_im, log_dt, b_re, b_im, c_re, c_im):
    g, n = lam_re.shape
    nblk = g // SSM_GB
    w = SSM_GB * n
    lam = lax.complex(lam_re, lam_im)
    dt = jnp.exp(log_dt)[:, None]
    lam_bar = jnp.exp(lam * dt)
    b_bar = ((lam_bar - 1.0) / lam)[:, :, None] * lax.complex(b_re, b_im)
    eye = jnp.eye(SSM_GB, dtype=F32)

    def blockdiag_in(x):
        x = x.reshape(nblk, SSM_GB, n, SSM_GROUP).transpose(0, 1, 3, 2)
        y = x[:, :, :, None, :] * eye[None, :, None, :, None]
        return y.reshape(nblk, SSM_GB * SSM_GROUP, w)

    def blockdiag_out(x):
        x = x.reshape(nblk, SSM_GB, SSM_GROUP, n).transpose(0, 3, 1, 2)
        y = x[:, None, :, :, :] * eye.T[None, :, None, :, None]
        return y.reshape(nblk, w, SSM_GB * SSM_GROUP)

    wb = jnp.concatenate([blockdiag_in(jnp.real(b_bar)), blockdiag_in(jnp.imag(b_bar))], axis=2)
    wc = jnp.concatenate([blockdiag_out(c_re), blockdiag_out(-c_im)], axis=1)

    def pack(x):
        lead = x.shape[:-2]
        x = x.reshape(lead + (nblk, w))
        return jnp.concatenate([jnp.real(x), jnp.imag(x)], axis=-1)

    lam_p = pack(lam_bar)
    lam8 = jnp.broadcast_to(lam_p[:, None, :], (nblk, SUBLANES, 2 * w))
    lam_sub = pack(jnp.exp(lam * dt * float(SSM_SUB)))
    lamr8 = jnp.broadcast_to(lam_sub[:, None, :], (nblk, SUBLANES, 2 * w))
    steps = jnp.arange(1, SSM_SUB + 1, dtype=F32)[:, None, None]
    pw = pack(jnp.exp((lam * dt)[None] * steps))
    pw = jnp.broadcast_to(pw.transpose(1, 0, 2)[:, :, None, :], (nblk, SSM_SUB, SUBLANES, 2 * w))
    pw = pw.reshape(nblk, SSM_CHUNK, 2 * w)
    return wb.astype(BF16), wc.astype(BF16), lam8, lamr8, pw


def _q_proj_body(qa_ref, g_ref, wn_ref, wpa_ref, wpb_ref, wuk_ref, tc_ref, ts_ref, qc_ref):
    x = qa_ref[...]
    xn = (x * lax.rsqrt(jnp.mean(x * x, axis=-1, keepdims=True) + EPS) * g_ref[...]).astype(BF16)
    q_nope = _dot(xn, wn_ref[...]).astype(BF16)
    pe = _dot(xn, wpa_ref[...]) * tc_ref[...] + _dot(xn, wpb_ref[...]) * ts_ref[...]
    low = lax.broadcasted_iota(jnp.int32, pe.shape, 1) < QK_ROPE
    qc_ref[0, :, :KV_RANK] = _dot(q_nope[:, :QK_NOPE], wuk_ref[0]).astype(BF16)
    qc_ref[1, :, :KV_RANK] = _dot(q_nope[:, QK_NOPE:], wuk_ref[1]).astype(BF16)
    qc_ref[0, :, KV_RANK:] = jnp.where(low, pe, 0.0).astype(BF16)
    qc_ref[1, :, KV_RANK:] = jnp.where(low, pltpu.roll(pe, QK_ROPE, 1), 0.0).astype(BF16)


def _q_proj(z, qa_col, q_norm_g, w_nope, w_pe_a, w_pe_b, w_uk_t, tab_c, tab_s):
    m = z.shape[0]
    tm = MM_TM
    hp = N_HEADS // 2
    return pl.pallas_call(
        _q_proj_body,
        grid=(hp, m // tm),
        in_specs=[pl.BlockSpec((tm, Q_RANK), lambda h, i: (i, qa_col // Q_RANK)),
                  pl.BlockSpec((1, Q_RANK), lambda h, i: (0, 0)),
                  pl.BlockSpec((Q_RANK, 2 * QK_NOPE), lambda h, i: (0, h)),
                  pl.BlockSpec((Q_RANK, 2 * QK_ROPE), lambda h, i: (0, h)),
                  pl.BlockSpec((Q_RANK, 2 * QK_ROPE), lambda h, i: (0, h)),
                  pl.BlockSpec((2, QK_NOPE, KV_RANK), lambda h, i: (h, 0, 0)),
                  pl.BlockSpec((tm, LANES), lambda h, i: (i, 0)),
                  pl.BlockSpec((tm, LANES), lambda h, i: (i, 0))],
        out_specs=pl.BlockSpec((2, tm, QK_CAT), lambda h, i: (h, i, 0)),
        out_shape=jax.ShapeDtypeStruct((N_HEADS, m, QK_CAT), BF16),
        compiler_params=_cparams(("arbitrary", "arbitrary")),
        name="q_proj",
    )(z, q_norm_g.reshape(1, Q_RANK), w_nope, w_pe_a, w_pe_b, w_uk_t, tab_c, tab_s)


def _kv_body(kv_ref, kp_ref, g_ref, tk_ref, ckv_ref, kpe_ref, kc_ref):
    x = kv_ref[...]
    c = x * lax.rsqrt(jnp.mean(x * x, axis=-1, keepdims=True) + EPS) * g_ref[...]
    ckv_ref[...] = c
    kc_ref[:, :KV_RANK] = c.astype(BF16)
    t = kp_ref[...] * tk_ref[...]
    r = t + pltpu.roll(t, QK_ROPE, 1)
    kpe_ref[...] = r[:, :QK_ROPE]
    lane = lax.broadcasted_iota(jnp.int32, r.shape, 1)
    kc_ref[:, KV_RANK:] = jnp.where(lane < QK_ROPE, r, 0.0).astype(BF16)


def _kv_proj(z, kv_col, kp_col, kv_norm_g, tab_k):
    m = z.shape[0]
    tm = MM_TM
    row = lambda width: pl.BlockSpec((tm, width), lambda i: (i, 0))
    return pl.pallas_call(
        _kv_body,
        grid=(m // tm,),
        in_specs=[pl.BlockSpec((tm, KV_RANK), lambda i: (i, kv_col // KV_RANK)),
                  pl.BlockSpec((tm, LANES), lambda i: (i, kp_col // LANES)),
                  pl.BlockSpec((1, KV_RANK), lambda i: (0, 0)),
                  row(LANES)],
        out_specs=[row(KV_RANK), row(QK_ROPE), row(QK_CAT)],
        out_shape=[jax.ShapeDtypeStruct((m, KV_RANK), F32),
                   jax.ShapeDtypeStruct((m, QK_ROPE), F32),
                   jax.ShapeDtypeStruct((m, QK_CAT), BF16)],
        compiler_params=_cparams(("arbitrary",)),
        name="kv_proj",
    )(z, z, kv_norm_g.reshape(1, KV_RANK), tab_k)


def _uv_body(olp_ref, ols_ref, w_ref, o_ref, *, prompt_tiles):
    def project(ol_ref):
        o_ref[:, :V_DIM] = _dot(ol_ref[0], w_ref[0]).astype(o_ref.dtype)
        o_ref[:, V_DIM:] = _dot(ol_ref[1], w_ref[1]).astype(o_ref.dtype)

    i = pl.program_id(1)
    pl.when(i < prompt_tiles)(lambda: project(olp_ref))
    pl.when(i >= prompt_tiles)(lambda: project(ols_ref))


def _uv_proj(ol_p, ol_s, w_uv_t):
    tm = MM_TM
    pt, st = ol_p.shape[1] // tm, ol_s.shape[1] // tm
    m = ol_p.shape[1] + ol_s.shape[1]
    return pl.pallas_call(
        functools.partial(_uv_body, prompt_tiles=pt),
        grid=(N_HEADS // 2, pt + st),
        in_specs=[pl.BlockSpec((2, tm, KV_RANK), lambda h, i: (h, jnp.minimum(i, pt - 1), 0)),
                  pl.BlockSpec((2, tm, KV_RANK), lambda h, i: (h, jnp.maximum(i - pt, 0), 0)),
                  pl.BlockSpec((2, KV_RANK, V_DIM), lambda h, i: (h, 0, 0))],
        out_specs=pl.BlockSpec((tm, 2 * V_DIM), lambda h, i: (i, h)),
        out_shape=jax.ShapeDtypeStruct((m, N_HEADS * V_DIM), BF16),
        compiler_params=_cparams(("arbitrary", "arbitrary")),
        name="uv_proj",
    )(ol_p, ol_s, w_uv_t)


def _softmax_update(s, v, m_s, l_s, acc_s):
    m_old = m_s[...]
    m_new = jnp.maximum(m_old, jnp.max(s, axis=-1, keepdims=True))
    alpha = jnp.exp(m_old - m_new)
    p = jnp.exp(s - m_new)
    l_s[...] = alpha * l_s[...] + jnp.sum(p, axis=-1, keepdims=True)
    acc_s[...] = alpha * acc_s[...] + _dot(p.astype(BF16), v)
    m_s[...] = m_new


def _softmax_update_t(s, v, m_s, l_s, acc_s):
    m_old = m_s[...]
    m_new = jnp.maximum(m_old, jnp.max(s, axis=0, keepdims=True))
    alpha = jnp.exp(m_old - m_new)
    p = jnp.exp(s - m_new)
    l_s[...] = alpha * l_s[...] + jnp.sum(p, axis=0, keepdims=True)
    pv = lax.dot_general(v, p.astype(BF16), (((0,), (0,)), ((), ())), preferred_element_type=F32)
    acc_s[...] = alpha * acc_s[...] + pv
    m_s[...] = m_new


def _attn_prompt_body(qb_ref, kb_ref, qi_ref, kj_ref, fl_ref, q_ref, k_ref, o_ref, m_s, l_s, acc_s, *, scale):
    st = pl.program_id(0)
    cols = N_HEADS * ATT_TQ

    @pl.when(kj_ref[st] == 0)
    def _():
        m_s[...] = jnp.full_like(m_s, -jnp.inf)
        l_s[...] = jnp.zeros_like(l_s)
        acc_s[...] = jnp.zeros_like(acc_s)

    q = q_ref[...].reshape(cols, QK_CAT)
    k = k_ref[...]
    s = _dot_nt(k, q) * scale
    v = k[:, :KV_RANK]
    last = fl_ref[st] != 0

    @pl.when(jnp.logical_not(last))
    def _():
        _softmax_update_t(s, v, m_s, l_s, acc_s)

    @pl.when(last)
    def _():
        k_tok = kj_ref[st] * ATT_TK + lax.broadcasted_iota(jnp.int32, s.shape, 0)
        q_tok = qi_ref[st] * ATT_TQ + lax.rem(lax.broadcasted_iota(jnp.int32, s.shape, 1), ATT_TQ)
        _softmax_update_t(jnp.where(k_tok <= q_tok, s, MASK_VALUE), v, m_s, l_s, acc_s)
        o_t = acc_s[...] / l_s[...]
        o_ref[...] = o_t.T.astype(o_ref.dtype).reshape(N_HEADS, ATT_TQ, KV_RANK)


def _attn_prompt(qc, kc, n_batch, seq, scale):
    assert ATT_TK % ATT_TQ == 0
    nq = seq // ATT_TQ
    nk = seq // ATT_TK
    qb, kb, qi, kj, fl = [], [], [], [], []
    for b in range(n_batch):
        for i in range(nq):
            jmax = (i * ATT_TQ + ATT_TQ - 1) // ATT_TK
            for j in range(jmax + 1):
                qb.append(b * nq + i), kb.append(b * nk + j), qi.append(i), kj.append(j), fl.append(int(j == jmax))
    tabs = [jnp.asarray(t, jnp.int32) for t in (qb, kb, qi, kj, fl)]
    cols = N_HEADS * ATT_TQ
    grid_spec = pltpu.PrefetchScalarGridSpec(
        num_scalar_prefetch=5,
        grid=(len(qb),),
        in_specs=[pl.BlockSpec((N_HEADS, ATT_TQ, QK_CAT), lambda s, qb, kb, qi, kj, fl: (0, qb[s], 0)),
                  pl.BlockSpec((ATT_TK, QK_CAT), lambda s, qb, kb, qi, kj, fl: (kb[s], 0))],
        out_specs=pl.BlockSpec((N_HEADS, ATT_TQ, KV_RANK), lambda s, qb, kb, qi, kj, fl: (0, qb[s], 0)),
        scratch_shapes=[pltpu.VMEM((1, cols), F32), pltpu.VMEM((1, cols), F32),
                        pltpu.VMEM((KV_RANK, cols), F32)])
    return pl.pallas_call(
        functools.partial(_attn_prompt_body, scale=scale),
        grid_spec=grid_spec,
        out_shape=jax.ShapeDtypeStruct((N_HEADS, n_batch * seq, KV_RANK), BF16),
        compiler_params=_cparams(("arbitrary",)),
        name="attn_prompt",
    )(*tabs, qc, kc)


def _attn_decode_body(pt_ref, qc_ref, kn_ref, cc_hbm, ck_hbm, o_ref,
                      m_s, l_s, acc_s, cbuf, kbuf, ck_s, kt_s, sem, *, scale, n_new, steps):
    g = pl.program_id(0)
    ng = pl.num_programs(0)
    j = lax.rem(g, steps)
    slot = lax.rem(g, 2)

    def group_copies(step, sl):
        b = step // steps
        first = lax.rem(step, steps) * DEC_PAGES
        copies = []
        for p in range(DEC_PAGES):
            page = pt_ref[b, first + p]
            copies.append(pltpu.make_async_copy(cc_hbm.at[page], cbuf.at[sl, p], sem.at[0, sl]))
            copies.append(pltpu.make_async_copy(ck_hbm.at[page], kbuf.at[sl, p], sem.at[1, sl]))
        return copies

    def wait_group(sl):
        pltpu.make_async_copy(cc_hbm.at[pl.ds(0, DEC_PAGES)], cbuf.at[sl], sem.at[0, sl]).wait()
        pltpu.make_async_copy(ck_hbm.at[pl.ds(0, DEC_PAGES)], kbuf.at[sl], sem.at[1, sl]).wait()

    @pl.when(g == 0)
    def _():
        for c in group_copies(g, slot):
            c.start()

    for c in group_copies(jnp.minimum(g + 1, ng - 1), 1 - slot):
        c.start()

    @pl.when(j == 0)
    def _():
        m_s[...] = jnp.full_like(m_s, -jnp.inf)
        l_s[...] = jnp.zeros_like(l_s)
        acc_s[...] = jnp.zeros_like(acc_s)

    wait_group(slot)
    for p in range(DEC_PAGES):
        ck_s[pl.ds(p * PAGE_SIZE, PAGE_SIZE), :] = cbuf[slot, p].astype(BF16)
        kt_s[:, pl.ds(p * PAGE_SIZE, PAGE_SIZE)] = kbuf[slot, p].astype(BF16)
    qc = qc_ref[...]
    ck = ck_s[...]
    s = (_dot_nt(qc[:, :KV_RANK], ck) + _dot(qc[:, KV_RANK:KV_RANK + QK_ROPE], kt_s[...])) * scale
    _softmax_update(s, ck, m_s, l_s, acc_s)

    @pl.when(g == ng - 1)
    def _():
        wait_group(1 - slot)

    @pl.when(j == steps - 1)
    def _():
        kn = kn_ref[...]
        s2 = _dot_nt(qc, kn) * scale
        q_tok = lax.rem(lax.broadcasted_iota(jnp.int32, s2.shape, 0), n_new)
        k_tok = lax.broadcasted_iota(jnp.int32, s2.shape, 1)
        _softmax_update(jnp.where(k_tok <= q_tok, s2, MASK_VALUE), kn[:, :KV_RANK], m_s, l_s, acc_s)
        o_ref[...] = (acc_s[...] / l_s[...]).astype(o_ref.dtype)


def _attn_decode(page_table, qc, kc_new, cache_ckv, cache_kpe_t, scale, n_new):
    n_seq, n_pages = page_table.shape
    rows = qc.shape[1]
    steps = n_pages // DEC_PAGES
    keys = DEC_PAGES * PAGE_SIZE
    per_seq = lambda r, width: pl.BlockSpec((None, r, width), lambda g, pt: (g // steps, 0, 0))
    grid_spec = pltpu.PrefetchScalarGridSpec(
        num_scalar_prefetch=1,
        grid=(n_seq * steps,),
        in_specs=[per_seq(rows, QK_CAT), per_seq(SUBLANES, QK_CAT),
                  pl.BlockSpec(memory_space=pl.ANY), pl.BlockSpec(memory_space=pl.ANY)],
        out_specs=per_seq(rows, KV_RANK),
        scratch_shapes=[pltpu.VMEM((rows, 1), F32), pltpu.VMEM((rows, 1), F32),
                        pltpu.VMEM((rows, KV_RANK), F32),
                        pltpu.VMEM((2, DEC_PAGES, PAGE_SIZE, KV_RANK), F32),
                        pltpu.VMEM((2, DEC_PAGES, QK_ROPE, PAGE_SIZE), F32),
                        pltpu.VMEM((keys, KV_RANK), BF16), pltpu.VMEM((QK_ROPE, keys), BF16),
                        pltpu.SemaphoreType.DMA((2, 2))])
    return pl.pallas_call(
        functools.partial(_attn_decode_body, scale=scale, n_new=n_new, steps=steps),
        grid_spec=grid_spec,
        out_shape=jax.ShapeDtypeStruct((n_seq, rows, KV_RANK), BF16),
        compiler_params=_cparams(("arbitrary",)),
        name="attn_decode",
    )(page_table, qc, kc_new, cache_ckv, cache_kpe_t)


def _router_body(h_ref, w_ref, b_ref, idx_ref, wt_ref):
    h = h_ref[...]
    w = w_ref[...]
    h_hi = h.astype(BF16)
    h_lo = (h - h_hi.astype(F32)).astype(BF16)
    w_hi = w.astype(BF16)
    w_lo = (w - w_hi.astype(F32)).astype(BF16)
    logits = _dot(h_hi, w_hi) + (_dot(h_hi, w_lo) + _dot(h_lo, w_hi)) + b_ref[...]
    lane = lax.broadcasted_iota(jnp.int32, logits.shape, 1).astype(F32)
    vals = jnp.where(lane < N_EXPERTS, logits, -jnp.inf)
    idx_out = jnp.zeros(logits.shape, F32)
    wt_out = jnp.zeros(logits.shape, F32)
    top = None
    den = jnp.zeros((logits.shape[0], 1), F32)
    for k in range(TOP_K):
        m = jnp.max(vals, axis=-1, keepdims=True)
        sel = jnp.min(jnp.where(vals == m, lane, float(LANES)), axis=-1, keepdims=True)
        if top is None:
            top = m
        e = jnp.exp(m - top)
        den = den + e
        idx_out = jnp.where(lane == k, sel, idx_out)
        wt_out = jnp.where(lane == k, e, wt_out)
        vals = jnp.where(lane == sel, -jnp.inf, vals)
    idx_ref[...] = idx_out.astype(jnp.int32)
    wt_ref[...] = wt_out / den


def _router(h2, w_router_p, b_router_p):
    m, d = h2.shape
    tm = MM_TM
    return pl.pallas_call(
        _router_body,
        grid=(m // tm,),
        in_specs=[pl.BlockSpec((tm, d), lambda i: (i, 0)),
                  pl.BlockSpec((d, LANES), lambda i: (0, 0)),
                  pl.BlockSpec((1, LANES), lambda i: (0, 0))],
        out_specs=[pl.BlockSpec((tm, LANES), lambda i: (i, 0)),
                   pl.BlockSpec((tm, LANES), lambda i: (i, 0))],
        out_shape=[jax.ShapeDtypeStruct((m, LANES), jnp.int32),
                   jax.ShapeDtypeStruct((m, LANES), F32)],
        compiler_params=_cparams(("arbitrary",)),
        name="router",
    )(h2, w_router_p, b_router_p)


def _moe_gather_body(tv_ref, tok_ref, h_hbm, o_ref, buf, sem):
    t = pl.program_id(0)

    def row_copy(src_row, dst_row):
        return pltpu.make_async_copy(h_hbm.at[pl.ds(src_row, 1), :], buf.at[pl.ds(dst_row, 1), :], sem)

    @pl.when(tv_ref[t] != 0)
    def _():
        def issue(r, carry):
            row_copy(tok_ref[0, r], r).start()
            return carry

        lax.fori_loop(0, MOE_TM, issue, 0)
        pltpu.make_async_copy(h_hbm.at[pl.ds(0, MOE_TM), :], buf, sem).wait()
        o_ref[...] = buf[...].astype(o_ref.dtype)

    @pl.when(tv_ref[t] == 0)
    def _():
        o_ref[...] = jnp.zeros_like(o_ref)


def _moe_gather(tile_v, tok_tiles, h2):
    n_tiles = tok_tiles.shape[0]
    d = h2.shape[1]
    grid_spec = pltpu.PrefetchScalarGridSpec(
        num_scalar_prefetch=1,
        grid=(n_tiles,),
        in_specs=[pl.BlockSpec((None, 1, MOE_TM), lambda t, tv: (t, 0, 0), memory_space=pltpu.SMEM),
                  pl.BlockSpec(memory_space=pl.ANY)],
        out_specs=pl.BlockSpec((MOE_TM, d), lambda t, tv: (t, 0)),
        scratch_shapes=[pltpu.VMEM((MOE_TM, d), F32), pltpu.SemaphoreType.DMA(())])
    return pl.pallas_call(
        _moe_gather_body,
        grid_spec=grid_spec,
        out_shape=jax.ShapeDtypeStruct((n_tiles * MOE_TM, d), BF16),
        compiler_params=_cparams(("arbitrary",)),
        name="moe_gather",
    )(tile_v, tok_tiles, h2)


def _moe_gu_body(te_ref, tv_ref, x_ref, wg_ref, wu_ref, bg_ref, bu_ref, o_ref):
    t = pl.program_id(1)

    @pl.when(tv_ref[t] != 0)
    def _():
        x = x_ref[...]
        g = _dot(x, wg_ref[...].astype(BF16)) + bg_ref[...]
        u = _dot(x, wu_ref[...].astype(BF16)) + bu_ref[...]
        gate = jnp.minimum(g, SWIGLU_LIMIT)
        up = jnp.clip(u, -SWIGLU_LIMIT, SWIGLU_LIMIT)
        o_ref[...] = ((up + 1.0) * gate * _sigmoid(SWIGLU_ALPHA * gate)).astype(o_ref.dtype)

    @pl.when(tv_ref[t] == 0)
    def _():
        o_ref[...] = jnp.zeros_like(o_ref)


def _moe_gu(tile_e, tile_v, xs, w_gu, b_gu):
    rows, d = xs.shape
    f = w_gu.shape[2] // 2
    nb = f // MOE_TN
    grid_spec = pltpu.PrefetchScalarGridSpec(
        num_scalar_prefetch=2,
        grid=(nb, rows // MOE_TM),
        in_specs=[pl.BlockSpec((MOE_TM, d), lambda n, t, te, tv: (t, 0)),
                  pl.BlockSpec((None, d, MOE_TN), lambda n, t, te, tv: (te[t], 0, n)),
                  pl.BlockSpec((None, d, MOE_TN), lambda n, t, te, tv: (te[t], 0, n + nb)),
                  pl.BlockSpec((None, 1, MOE_TN), lambda n, t, te, tv: (te[t], 0, n)),
                  pl.BlockSpec((None, 1, MOE_TN), lambda n, t, te, tv: (te[t], 0, n + nb))],
        out_specs=pl.BlockSpec((MOE_TM, MOE_TN), lambda n, t, te, tv: (t, n)))
    return pl.pallas_call(
        _moe_gu_body,
        grid_spec=grid_spec,
        out_shape=jax.ShapeDtypeStruct((rows, f), BF16),
        compiler_params=_cparams(("arbitrary", "arbitrary")),
        name="moe_gate_up",
    )(tile_e, tile_v, xs, w_gu, w_gu, b_gu, b_gu)


def _moe_down_body(te_ref, tv_ref, a_ref, w_ref, b_ref, rw_ref, o_ref):
    t = pl.program_id(1)

    @pl.when(tv_ref[t] != 0)
    def _():
        o_ref[...] = (_dot(a_ref[...], w_ref[...].astype(BF16)) + b_ref[...]) * rw_ref[...]

    @pl.when(tv_ref[t] == 0)
    def _():
        o_ref[...] = jnp.zeros_like(o_ref)


def _moe_down(tile_e, tile_v, act, w_down, b_down, row_w):
    rows, f = act.shape
    d = w_down.shape[2]
    nb = d // MOE_TN
    grid_spec = pltpu.PrefetchScalarGridSpec(
        num_scalar_prefetch=2,
        grid=(nb, rows // MOE_TM),
        in_specs=[pl.BlockSpec((MOE_TM, f), lambda n, t, te, tv: (t, 0)),
                  pl.BlockSpec((None, f, MOE_TN), lambda n, t, te, tv: (te[t], 0, n)),
                  pl.BlockSpec((None, 1, MOE_TN), lambda n, t, te, tv: (te[t], 0, n)),
                  pl.BlockSpec((MOE_TM, 1), lambda n, t, te, tv: (t, 0))],
        out_specs=pl.BlockSpec((MOE_TM, MOE_TN), lambda n, t, te, tv: (t, n)))
    return pl.pallas_call(
        _moe_down_body,
        grid_spec=grid_spec,
        out_shape=jax.ShapeDtypeStruct((rows, d), F32),
        compiler_params=_cparams(("arbitrary", "arbitrary")),
        name="moe_down",
    )(tile_e, tile_v, act, w_down, b_down, row_w)


def _moe(h2, w_router, b_router, w_gu, b_gu, w_down, b_down):
    n, d = h2.shape
    w_router_p = jnp.pad(w_router, ((0, 0), (0, LANES - N_EXPERTS)))
    b_router_p = jnp.pad(b_router, (0, LANES - N_EXPERTS)).reshape(1, LANES)
    idx_p, wt_p = _router(h2, w_router_p, b_router_p)
    top_i = idx_p[:, :TOP_K]
    wts = wt_p[:, :TOP_K]

    n_rows = n * TOP_K
    rows_pad = n_rows + N_EXPERTS * MOE_TM
    n_tiles = rows_pad // MOE_TM
    flat_e = top_i.reshape(-1)
    order = jnp.argsort(flat_e)
    e_sorted = flat_e[order]
    sizes = jnp.bincount(flat_e, length=N_EXPERTS).astype(jnp.int32)
    psizes = ((sizes + MOE_TM - 1) // MOE_TM) * MOE_TM
    pend = jnp.cumsum(psizes)
    pstart = pend - psizes
    start = jnp.cumsum(sizes) - sizes
    dest = pstart[e_sorted] + (jnp.arange(n_rows, dtype=jnp.int32) - start[e_sorted])
    tok_pad = jnp.zeros((rows_pad,), jnp.int32).at[dest].set((order // TOP_K).astype(jnp.int32))
    w_pad = jnp.zeros((rows_pad,), F32).at[dest].set(wts.reshape(-1)[order])
    pos = jnp.zeros((n_rows,), jnp.int32).at[order].set(dest).reshape(n, TOP_K)
    tile_start = jnp.arange(n_tiles, dtype=jnp.int32) * MOE_TM
    tile_e = jnp.minimum(jnp.searchsorted(pend, tile_start, side='right'), N_EXPERTS - 1).astype(jnp.int32)
    tile_v = (tile_start < pend[-1]).astype(jnp.int32)

    xs = _moe_gather(tile_v, tok_pad.reshape(n_tiles, 1, MOE_TM), h2)
    act =_moe_gu(tile_e, tile_v, xs, w_gu, b_gu.reshape(N_EXPERTS, 1, -1))
    eo = _moe_down(tile_e, tile_v, act, w_down, b_down.reshape(N_EXPERTS, 1, -1), w_pad.reshape(rows_pad, 1))
    return eo[pos].sum(axis=1)


def kernel(x_prompt, x_sample, c_prompt, c_sample, cache_ckv, cache_kpe, page_table, state_ssm_re, state_ssm_im, w_ada, b_ada, g_mix, w_in, ssm_lam_re, ssm_lam_im, ssm_log_dt, ssm_b_re, ssm_b_im, ssm_c_re, ssm_c_im, ssm_d, w_glu, q_norm_g, kv_norm_g, w_qb, w_uk, w_uv, w_bo, w_o, g_ffn, w_router, b_router, w_gu, b_gu, w_down, b_down, g_final):
    n_batch, seq, d = x_prompt.shape
    n_seq, n_new, _ = x_sample.shape
    depth = w_ada.shape[0]
    d_ssm = ssm_d.shape[1]
    n_groups = d_ssm // SSM_GROUP
    assert depth == 1 and n_seq == ROW_TILE and seq % SSM_CHUNK == 0 and seq % MM_TM == 0
    assert n_new <= SUBLANES and page_table.shape[1] % DEC_PAGES == 0
    np_rows = n_batch * seq
    ns_rows = n_seq * n_new
    n_rows = np_rows + ns_rows
    assert n_rows % MM_TM == 0 and ns_rows == MM_TM
    tiles_per_batch = seq // ROW_TILE
    past_len = page_table.shape[1] * PAGE_SIZE
    scale = float((QK_NOPE + QK_ROPE) ** -0.5)

    x_all = jnp.concatenate([x_prompt.reshape(np_rows, d),
                             x_sample.transpose(1, 0, 2).reshape(ns_rows, d)], axis=0)
    c_all = jnp.concatenate([c_prompt, c_sample], axis=0)

    w_ada_b = w_ada[0].astype(BF16)
    sp = (d_ssm, d_ssm + Q_RANK, d_ssm + Q_RANK + KV_RANK, d_ssm + Q_RANK + KV_RANK + QK_ROPE)
    half = QK_ROPE // 2
    swap = jnp.concatenate([jnp.arange(half, QK_ROPE), jnp.arange(0, half)])
    w_in0 = w_in[0]
    kp_col = sp[2]
    ga_col = kp_col + 2 * LANES
    gb_col = ga_col + d
    w_in_r = jnp.concatenate([w_in0[:, :sp[3]], w_in0[:, sp[2] + swap],
                              jnp.zeros((d, ga_col - sp[3] - QK_ROPE), F32), w_in0[:, sp[3]:]], axis=1).astype(BF16)
    assert ga_col % 512 == 0 and w_in_r.shape[1] % 512 == 0
    w_qb_h = w_qb[0].reshape(Q_RANK, N_HEADS, QK_NOPE + QK_ROPE)
    w_nope = w_qb_h[:, :, :QK_NOPE].reshape(Q_RANK, N_HEADS * QK_NOPE).astype(BF16)
    w_pe_a = w_qb_h[:, :, QK_NOPE:].reshape(Q_RANK, N_HEADS * QK_ROPE).astype(BF16)
    w_pe_b = w_qb_h[:, :, QK_NOPE + swap].reshape(Q_RANK, N_HEADS * QK_ROPE).astype(BF16)
    w_uk_t = w_uk[0].transpose(1, 2, 0).astype(BF16)
    w_uv_t = w_uv[0].transpose(1, 0, 2).astype(BF16)
    w_glu_b = w_glu[0].astype(BF16)
    w_bo_b = w_bo[0].astype(BF16)
    w_o_b = w_o[0].astype(BF16)
    wb, wc, lam8, lamr8, pw = _ssm_params(ssm_lam_re[0], ssm_lam_im[0], ssm_log_dt[0], ssm_b_re[0], ssm_b_im[0],
                                          ssm_c_re[0], ssm_c_im[0])
    d_skip = ssm_d[0].reshape(1, d_ssm)

    inv = ROPE_THETA ** (-jnp.arange(half, dtype=F32) / half)
    pos = jnp.concatenate([jnp.tile(jnp.arange(seq, dtype=jnp.int32), n_batch),
                           jnp.repeat(past_len + jnp.arange(n_new, dtype=jnp.int32), n_seq)])
    ang = pos.astype(F32)[:, None] * inv
    cos, sin = jnp.cos(ang), jnp.sin(ang)
    tab_c = jnp.concatenate([cos, cos, cos, cos], axis=1)
    tab_s = jnp.concatenate([-sin, sin, -sin, sin], axis=1)
    tab_k = jnp.concatenate([cos, cos, -sin, sin], axis=1)

    mod = _ada(c_all, w_ada_b, b_ada[0])
    mod_all = jnp.concatenate([jnp.broadcast_to(mod[:n_batch, None, :], (n_batch, ROW_TILE, 6 * d)),
                               mod[n_batch:][None]], axis=0)

    h = _norm_mod(x_all, g_mix[0], mod_all, tiles_per_batch, n_batch)
    z = _matmul(h, w_in_r, tn=512, out_dtype=F32, name="in_proj")

    gy_p, st_p = _ssm_prompt(z, n_batch, seq, wb, wc, d_skip, lam8, lamr8, pw)
    h0_re = state_ssm_re[0].reshape(n_seq, n_groups * SSM_STATE)
    h0_im = state_ssm_im[0].reshape(n_seq, n_groups * SSM_STATE)
    gy_s, sre_s, sim_s = _ssm_sample(z, np_rows, n_new, n_seq, wb, wc, d_skip, lam8, h0_re, h0_im)
    gy = jnp.concatenate([gy_p, gy_s], axis=0)
    gated_a = _glu_branch(gy, w_glu_b, z, ga_col)

    qc = _q_proj(z, sp[0], q_norm_g[0], w_nope, w_pe_a, w_pe_b, w_uk_t, tab_c, tab_s)
    ckv, kpe, kc = _kv_proj(z, sp[1], kp_col, kv_norm_g[0], tab_k)

    ol_p = _attn_prompt(qc, kc, n_batch, seq, scale)

    qc_s = qc[:, np_rows:].reshape(N_HEADS, n_new, n_seq, QK_CAT).transpose(2, 0, 1, 3)
    qc_s = qc_s.reshape(n_seq, N_HEADS * n_new, QK_CAT)
    kc_s = kc[np_rows:].reshape(n_new, n_seq, QK_CAT).transpose(1, 0, 2)
    kc_s = jnp.pad(kc_s, ((0, 0), (0, SUBLANES - n_new), (0, 0)))
    ol_s = _attn_decode(page_table, qc_s, kc_s, cache_ckv[0], jnp.swapaxes(cache_kpe[0], 1, 2), scale, n_new)
    ol_s = ol_s.reshape(n_seq, N_HEADS, n_new, KV_RANK).transpose(1, 2, 0, 3).reshape(N_HEADS, ns_rows, KV_RANK)
    o = _uv_proj(ol_p, ol_s, w_uv_t)
    mix = _bo_mix(o, w_bo_b, gated_a, z, gb_col)
    mixed = _matmul(mix, w_o_b, tn=512, out_dtype=F32, name="out_proj")

    x1, h2 = _resid_norm_mod(x_all, mixed, g_ffn[0], mod_all, tiles_per_batch, n_batch)
    moe_out = _moe(h2, w_router[0], b_router[0], w_gu[0], b_gu[0], w_down[0], b_down[0])
    y = _final(x1, moe_out, g_final, mod_all, tiles_per_batch, n_batch)

    def sample_rows(x):
        return x[np_rows:].reshape(n_new, n_seq, -1).transpose(1, 0, 2)

    w = n_groups * SSM_STATE // wb.shape[0]
    st_p = st_p.reshape(n_batch, wb.shape[0], 2, SSM_GB, SSM_STATE)
    sre_p = st_p[:, :, 0].reshape(1, n_batch, n_groups, SSM_STATE)
    sim_p = st_p[:, :, 1].reshape(1, n_batch, n_groups, SSM_STATE)
    return (y[:np_rows].reshape(n_batch, seq, d),
            sample_rows(y),
            ckv[:np_rows].reshape(1, n_batch, seq, KV_RANK),
            kpe[:np_rows].reshape(1, n_batch, seq, QK_ROPE),
            sre_p, sim_p,
            sample_rows(ckv)[None], sample_rows(kpe)[None],
            sre_s.reshape(1, n_seq, n_groups, SSM_STATE), sim_s.reshape(1, n_seq, n_groups, SSM_STATE))
```
